```python
import math
import jax, jax.numpy as jnp
from jax import lax
import numpy as np

D_MODEL = 1024
BATCH = 32
SEQ = 2048
DEPTH = 4

ATTN_HEADS = 8
HEAD_DIM = 64
ATTN_WIDTH = ATTN_HEADS * HEAD_DIM
DILATED_PATTERNS = ((128, 1), (512, 4), (2048, 16))
REL_BUCKETS = 32
REL_MAX_DISTANCE = 1024
SSD_HEADS = 16
SSD_HEADDIM = 64
SSD_INNER = SSD_HEADS * SSD_HEADDIM
SSD_GROUPS = 2
SSD_STATE = 128
SSD_XBC = SSD_INNER + 2 * SSD_GROUPS * SSD_STATE
SSD_CONV = 7
SSD_CHUNK = 128
SC_WIDTH = 512
SC_CONV = 3
MIX_WIDTH = ATTN_WIDTH + SSD_INNER + SC_WIDTH
N_IN = 3 * ATTN_WIDTH + SSD_INNER + SSD_XBC + 2 * SSD_HEADS + 3 * SC_WIDTH
D_FF = 2816
FFN_CONV = 3
EPS = 1e-6
NEG_INF = -1e30

kernel_name = "hybrid_dilated_ssd_shortconv_encoder"


def rmsnorm(x, g):
    xf = x.astype(jnp.float32)
    y = xf * lax.rsqrt(jnp.mean(xf * xf, axis=-1, keepdims=True) + EPS)
    return (y * g.astype(jnp.float32)).astype(x.dtype)


def dwconv(x, w, b):
    k = w.shape[0]
    pad = k // 2
    y = lax.conv_general_dilated(x, w[:, None, :].astype(x.dtype), window_strides=(1,),
                                 padding=[(pad, pad)], dimension_numbers=('NWC', 'WIO', 'NWC'),
                                 feature_group_count=x.shape[-1])
    return y + b.astype(x.dtype)


def t5_bucket(rel):
    nb = REL_BUCKETS // 2
    max_exact = nb // 2
    ret = jnp.where(rel > 0, nb, 0)
    n = jnp.abs(rel)
    nf = jnp.maximum(n, 1).astype(jnp.float32)
    large = max_exact + (jnp.log(nf / max_exact) / math.log(REL_MAX_DISTANCE / max_exact)
                         * (nb - max_exact)).astype(jnp.int32)
    large = jnp.minimum(large, nb - 1)
    return ret + jnp.where(n < max_exact, n, large)


def dilated_attention(q, k, v, rel_table, window, dil):
    b, s, h, e = q.shape
    r = (window // 2) // dil
    blk = r
    L = s // dil
    nb = -(-L // blk)
    lp = nb * blk

    def to_sub(t, lo, hi):
        t = t.reshape(b, L, dil, h, e).transpose(0, 2, 1, 3, 4)
        return jnp.pad(t, ((0, 0), (0, 0), (lo, hi), (0, 0), (0, 0)))

    qb = to_sub(q, 0, lp - L).reshape(b, dil, nb, blk, h, e)

    def band(t):
        tb = to_sub(t, blk, lp - L + blk).reshape(b, dil, nb + 2, blk, h, e)
        return jnp.concatenate([tb[:, :, :-2], tb[:, :, 1:-1], tb[:, :, 2:]], axis=3)

    kb, vb = band(k), band(v)
    qi = jnp.arange(blk)[:, None]
    ki = jnp.arange(3 * blk)[None, :] - blk
    rel = ki - qi
    bias = rel_table[t5_bucket(rel * dil)].astype(jnp.float32).transpose(2, 0, 1)
    kpos = jnp.arange(nb)[:, None, None] * blk + ki[None]
    valid = (jnp.abs(rel) <= r)[None] & (kpos >= 0) & (kpos < L)

    scores = jnp.einsum('brnqhe,brnkhe->brhnqk', qb, kb) + bias[:, None]
    scores = jnp.where(valid, scores, NEG_INF)
    m = jnp.max(scores, axis=-1, keepdims=True)
    p = jnp.exp(scores - m)
    den = jnp.sum(p, axis=-1)
    o = jnp.einsum('brhnqk,brnkhe->brnqhe', p, vb) / jnp.moveaxis(den, 2, -1)[..., None]
    lse = jnp.moveaxis(m[..., 0] + jnp.log(den), 2, -1)

    def from_sub(t):
        t = t.reshape((b, dil, lp) + t.shape[4:])[:, :, :L]
        return jnp.swapaxes(t, 1, 2).reshape((b, s) + t.shape[3:])

    return from_sub(o), from_sub(lse)


def attention_mixer(q, k, v, q_g, k_g, rel_table):
    b, s, _ = q.shape
    heads = lambda t: t.reshape(b, s, ATTN_HEADS, HEAD_DIM)
    qf = rmsnorm(heads(q), q_g).astype(jnp.float32) * (HEAD_DIM ** -0.5)
    kf = rmsnorm(heads(k), k_g).astype(jnp.float32)
    vf = heads(v).astype(jnp.float32)
    res = [dilated_attention(qf, kf, vf, rel_table, w, d) for (w, d) in DILATED_PATTERNS]
    outs = jnp.stack([o for (o, _) in res])
    lses = jnp.stack([l for (_, l) in res])
    wts = jax.nn.softmax(lses, axis=0)
    o = jnp.sum(wts[..., None] * outs, axis=0)
    return o.reshape(b, s, ATTN_WIDTH).astype(q.dtype)


def ssd_chunked(xs, dt, a, bm, cm):
    b, s, h, p = xs.shape
    g, n = bm.shape[2], bm.shape[3]
    e = h // g
    t = SSD_CHUNK
    c = s // t
    X = (xs * dt[..., None]).reshape(b, c, t, g, e, p)
    a_cs = jnp.cumsum((dt * a).reshape(b, c, t, g, e), axis=2)
    Bc = bm.reshape(b, c, t, g, n)
    Cc = cm.reshape(b, c, t, g, n)
    lower = jnp.tril(jnp.ones((t, t), dtype=bool))[None, None, :, :, None, None]
    seg = a_cs[:, :, :, None] - a_cs[:, :, None, :]
    decay = jnp.exp(jnp.where(lower, seg, -jnp.inf))
    cb = jnp.einsum('bclgn,bcsgn->bclsg', Cc, Bc)
    y_diag = jnp.einsum('bclsge,bcsgep->bclgep', cb[..., None] * decay, X)
    x_to_end = X * jnp.exp(a_cs[:, :, -1:] - a_cs)[..., None]
    states = jnp.einsum('bcsgn,bcsgep->bcgepn', Bc, x_to_end)
    chunk_decay = jnp.exp(a_cs[:, :, -1])

    def carry_state(state, inp):
        st, dec = inp
        return state * dec[..., None, None] + st, state

    _, s_in = lax.scan(carry_state, jnp.zeros_like(states[:, 0]),
                       (jnp.moveaxis(states, 1, 0), jnp.moveaxis(chunk_decay, 1, 0)))
    s_in = jnp.moveaxis(s_in, 0, 1)
    y_off = jnp.einsum('bclgn,bcgepn->bclgep', Cc, s_in) * jnp.exp(a_cs)[..., None]
    return (y_diag + y_off).reshape(b, s, h, p)


def ssd_mixer(z, xbc, dt_raw, conv_w, conv_b, dt_bias, a_log, d_skip, norm_g):
    b, s, _ = z.shape
    f32 = jnp.float32
    xbc = jax.nn.silu(dwconv(xbc, conv_w, conv_b)).astype(f32)
    xs, bm, cm = jnp.split(xbc, [SSD_INNER, SSD_INNER + SSD_GROUPS * SSD_STATE], axis=-1)
    xs = xs.reshape(b, s, SSD_HEADS, SSD_HEADDIM)
    bm = bm.reshape(b, s, SSD_GROUPS, SSD_STATE)
    cm = cm.reshape(b, s, SSD_GROUPS, SSD_STATE)
    dt = jax.nn.softplus(dt_raw.astype(f32).reshape(b, s, 2, SSD_HEADS) + dt_bias.astype(f32))
    a = -jnp.exp(a_log.astype(f32))
    flip = lambda t: jnp.flip(t, axis=1)
    y_fwd = ssd_chunked(xs, dt[:, :, 0], a[0], bm, cm)
    y_bwd = flip(ssd_chunked(flip(xs), flip(dt[:, :, 1]), a[1], flip(bm), flip(cm)))
    y = y_fwd + y_bwd + d_skip.astype(f32)[:, None] * xs
    gsz = SSD_INNER // SSD_GROUPS
    y = y.reshape(b, s, SSD_GROUPS, gsz) * jax.nn.silu(z.astype(f32)).reshape(b, s, SSD_GROUPS, gsz)
    y = y * lax.rsqrt(jnp.mean(y * y, axis=-1, keepdims=True) + EPS)
    y = y.reshape(b, s, SSD_INNER) * norm_g.astype(f32)
    return y.astype(z.dtype)


def short_conv_mixer(gate_b, gate_c, xin, conv_w, conv_b):
    return gate_b * dwconv(gate_c * xin, conv_w, conv_b)


def conv_ffn(h, w_up, conv_w, conv_b, w_down):
    u = dwconv(h @ w_up, conv_w, conv_b)
    gate, val = jnp.split(u, 2, axis=-1)
    return (jax.nn.silu(gate) * val) @ w_down


def setup_inputs(seed: int = 0) -> dict:
    key = jax.random.key(seed)
    ks = jax.random.split(key, 24)
    nrm = lambda k, shape: jax.random.normal(k, shape, jnp.float32)
    dt0 = jnp.exp(jax.random.uniform(ks[8], (DEPTH, 2, SSD_HEADS), jnp.float32,
                                     math.log(1e-3), math.log(1e-1)))
    return {
        "x": nrm(ks[0], (BATCH, SEQ, D_MODEL)),
        "rel_table": 0.5 * nrm(ks[1], (REL_BUCKETS, ATTN_HEADS)),
        "norm1_g": 1.0 + 0.1 * nrm(ks[2], (DEPTH, D_MODEL)),
        "w_in": nrm(ks[3], (DEPTH, D_MODEL, N_IN)) * D_MODEL ** -0.5,
        "q_norm_g": 1.0 + 0.1 * nrm(ks[4], (DEPTH, HEAD_DIM)),
        "k_norm_g": 1.0 + 0.1 * nrm(ks[5], (DEPTH, HEAD_DIM)),
        "ssd_conv_w": nrm(ks[6], (DEPTH, SSD_CONV, SSD_XBC)) * SSD_CONV ** -0.5,
        "ssd_conv_b": 0.01 * nrm(ks[7], (DEPTH, SSD_XBC)),
        "ssd_dt_bias": dt0 + jnp.log(-jnp.expm1(-dt0)),
        "ssd_a_log": jnp.log(jax.random.uniform(ks[9], (DEPTH, 2, SSD_HEADS), jnp.float32, 1.0, 16.0)),
        "ssd_d": 1.0 + 0.1 * nrm(ks[10], (DEPTH, SSD_HEADS)),
        "ssd_norm_g": 1.0 + 0.1 * nrm(ks[11], (DEPTH, SSD_INNER)),
        "sc_conv_w": nrm(ks[12], (DEPTH, SC_CONV, SC_WIDTH)) * SC_CONV ** -0.5,
        "sc_conv_b": 0.01 * nrm(ks[13], (DEPTH, SC_WIDTH)),
        "w_out": nrm(ks[14], (DEPTH, MIX_WIDTH, D_MODEL)) * MIX_WIDTH ** -0.5,
        "norm2_g": 1.0 + 0.1 * nrm(ks[15], (DEPTH, D_MODEL)),
        "ffn_w_up": nrm(ks[16], (DEPTH, D_MODEL, 2 * D_FF)) * D_MODEL ** -0.5,
        "ffn_conv_w": nrm(ks[17], (DEPTH, FFN_CONV, 2 * D_FF)) * FFN_CONV ** -0.5,
        "ffn_conv_b": 0.01 * nrm(ks[18], (DEPTH, 2 * D_FF)),
        "ffn_w_down": nrm(ks[19], (DEPTH, D_FF, D_MODEL)) * D_FF ** -0.5,
    }


def reference(x, rel_table, norm1_g, w_in, q_norm_g, k_norm_g, ssd_conv_w, ssd_conv_b,
              ssd_dt_bias, ssd_a_log, ssd_d, ssd_norm_g, sc_conv_w, sc_conv_b, w_out,
              norm2_g, ffn_w_up, ffn_conv_w, ffn_conv_b, ffn_w_down):
    split_at = list(np.cumsum([ATTN_WIDTH, ATTN_WIDTH, ATTN_WIDTH, SSD_INNER, SSD_XBC,
                               2 * SSD_HEADS, SC_WIDTH, SC_WIDTH]))
    for i in range(DEPTH):
        h = rmsnorm(x, norm1_g[i])
        u = h @ w_in[i]
        q, k, v, z, xbc, dt_raw, sc_b, sc_c, sc_x = jnp.split(u, split_at, axis=-1)
        a_out = attention_mixer(q, k, v, q_norm_g[i], k_norm_g[i], rel_table)
        s_out = ssd_mixer(z, xbc, dt_raw, ssd_conv_w[i], ssd_conv_b[i], ssd_dt_bias[i],
                          ssd_a_log[i], ssd_d[i], ssd_norm_g[i])
        c_out = short_conv_mixer(sc_b, sc_c, sc_x, sc_conv_w[i], sc_conv_b[i])
        x = x + jnp.concatenate([a_out, s_out, c_out], axis=-1) @ w_out[i]
        x = x + conv_ffn(rmsnorm(x, norm2_g[i]), ffn_w_up[i], ffn_conv_w[i], ffn_conv_b[i], ffn_w_down[i])
    return x
```

```python
import functools
import math

import numpy as np
import jax
import jax.numpy as jnp
from jax import lax
from jax.experimental import pallas as pl
from jax.experimental.pallas import tpu as pltpu

F32 = jnp.float32
BF16 = jnp.bfloat16

LANES = 128
VMEM_LIMIT_BYTES = 56 * 1024 * 1024

D_MODEL = 1024
ATTN_HEADS = 8
HEAD_DIM = 64
ATTN_WIDTH = ATTN_HEADS * HEAD_DIM
DILATED_PATTERNS = ((128, 1), (512, 4), (2048, 16))
BAND = 64
REL_BUCKETS = 32
REL_MAX_DISTANCE = 1024
SSD_HEADS = 16
SSD_HEADDIM = 64
SSD_INNER = SSD_HEADS * SSD_HEADDIM
SSD_GROUPS = 2
SSD_STATE = 128
SSD_XBC = SSD_INNER + 2 * SSD_GROUPS * SSD_STATE
SSD_CONV = 7
SSD_CHUNK = 128
GROUP_HEADS = SSD_HEADS // SSD_GROUPS
GROUP_WIDTH = SSD_INNER // SSD_GROUPS
GROUP_PAIRS = GROUP_WIDTH // LANES
SC_WIDTH = 512
D_FF = 2816
EPS = 1e-6
NEG_INF = -1e30

QKV_W = 3 * ATTN_WIDTH
DT_W = SSD_GROUPS * LANES
SC_W = 3 * SC_WIDTH
IN_PIECES = (QKV_W, SSD_INNER, SSD_XBC, DT_W, SC_W)

IN_TM = 256
OUT_TM = 512
FFN_TM = 512
FFN_TF = 256
HALO = 8
QBLK = 128
KWIN = 2 * QBLK


def _split3(x):
    p0 = x.astype(BF16)
    r = x - p0.astype(F32)
    p1 = r.astype(BF16)
    p2 = (r - p1.astype(F32)).astype(BF16)
    return p0, p1, p2


def _dot(a, b):
    return jnp.dot(a, b, preferred_element_type=F32)


def _dot_nt(a, b):
    return lax.dot_general(a, b, (((1,), (1,)), ((), ())), preferred_element_type=F32)


def _silu(x):
    return x / (1.0 + jnp.exp(-x))


def _inproj_body(x_ref, g_ref, w_ref, qkv_ref, z_ref, xbc_ref, dt_ref, sc_ref):
    x = x_ref[...]
    ms = jnp.mean(x * x, axis=-1, keepdims=True)
    hb = (x * lax.rsqrt(ms + EPS) * g_ref[...]).astype(BF16)
    off = 0
    for ref in (qkv_ref, z_ref, xbc_ref, dt_ref, sc_ref):
        n = ref.shape[-1]
        ref[...] = _dot(hb, w_ref[:, off:off + n])
        off += n


def _in_proj(x2, g, w):
    m = x2.shape[0]
    n_total = sum(IN_PIECES)
    return pl.pallas_call(
        _inproj_body,
        grid=(m // IN_TM,),
        in_specs=[
            pl.BlockSpec((IN_TM, D_MODEL), lambda i: (i, 0)),
            pl.BlockSpec((1, D_MODEL), lambda i: (0, 0)),
            pl.BlockSpec((D_MODEL, n_total), lambda i: (0, 0), pipeline_mode=pl.Buffered(1)),
        ],
        out_specs=[pl.BlockSpec((IN_TM, n), lambda i: (i, 0)) for n in IN_PIECES],
        out_shape=[jax.ShapeDtypeStruct((m, n), F32) for n in IN_PIECES],
        compiler_params=pltpu.CompilerParams(
            dimension_semantics=("parallel",), vmem_limit_bytes=VMEM_LIMIT_BYTES),
        name="in_proj",
    )(x2, g, w)


def _attn_body(q_ref, k_ref, v_ref, gq_ref, gk_ref, bw_ref, bn_ref, o_ref,
               qn, kn, qp0, qp1, kp, vp, acc_p, m_p, l_p, a_run, m_run, l_run):
    s_len = q_ref.shape[0]
    lane = lax.broadcasted_iota(jnp.int32, (1, LANES), 1)
    head0 = lane < HEAD_DIM
    rr = lax.broadcasted_iota(jnp.int32, (LANES, LANES), 0)
    cc = lax.broadcasted_iota(jnp.int32, (LANES, LANES), 1)
    same_head = jnp.where((rr < HEAD_DIM) == (cc < HEAD_DIM), 1.0, 0.0).astype(BF16)

    def normalize(src, g_ref, dst):
        def body(i, carry):
            rows = pl.ds(pl.multiple_of(i * 256, 256), 256)
            x = src[rows, :]
            sq = x * x
            hi = sq.astype(BF16)
            lo = (sq - hi.astype(F32)).astype(BF16)
            ss = _dot(hi, same_head) + _dot(lo, same_head)
            dst[rows, :] = x * lax.rsqrt(ss * (1.0 / HEAD_DIM) + EPS) * g_ref[...]
            return carry
        lax.fori_loop(0, s_len // 256, body, 0)

    normalize(q_ref, gq_ref, qn)
    normalize(k_ref, gk_ref, kn)

    def residue_rows(d, r, c0, n):
        return pl.ds(c0, n) if d == 1 else pl.ds(r + c0 * d, n, stride=d)

    def permute(d):
        sub = s_len // d
        step = min(sub, 256)
        for r in range(d):
            for c0 in range(0, sub, step):
                src = residue_rows(d, r, c0, step)
                dst = pl.ds(r * sub + c0, step)
                qv = qn[src, :]
                qp0[dst, :] = jnp.where(head0, qv, 0.0).astype(BF16)
                qp1[dst, :] = jnp.where(head0, 0.0, qv).astype(BF16)
                kp[dst, :] = kn[src, :].astype(BF16)
                vp[dst, :] = v_ref[src, :].astype(BF16)

    def softmax_unit(qrows, kw, vw, bias_of_head):
        outs = []
        for h, qp in enumerate((qp0, qp1)):
            s = _dot_nt(qp[qrows, :], kw) + bias_of_head(h)
            m = jnp.max(s, axis=-1, keepdims=True)
            p = jnp.exp(s - m)
            l = jnp.sum(p, axis=-1, keepdims=True)
            outs.append((_dot(p.astype(BF16), vw), m, l))
        (a0, m0, l0), (a1, m1, l1) = outs
        acc_p[qrows, :] = jnp.where(head0, a0, a1)
        m_p[qrows, :] = jnp.where(head0, m0, m1)
        l_p[qrows, :] = jnp.where(head0, l0, l1)

    def attend_wide(pi, d):
        sub = s_len // d
        nb = sub // QBLK

        def unit(u, carry):
            t = u % nb
            qrows = pl.ds(pl.multiple_of(u * QBLK, QBLK), QBLK)
            kstart = (u - t) * QBLK + jnp.clip(t * QBLK - BAND, 0, sub - KWIN)
            krows = pl.ds(pl.multiple_of(kstart, BAND), KWIN)
            var = jnp.where(t == 0, 0, jnp.where(t == nb - 1, 2, 1))
            softmax_unit(qrows, kp[krows, :], vp[krows, :], lambda h: bw_ref[pi, var, h])
            return carry
        lax.fori_loop(0, s_len // QBLK, unit, 0)

    def attend_narrow():
        def unit(u, carry):
            rows = pl.ds(pl.multiple_of(u * QBLK, QBLK), QBLK)
            softmax_unit(rows, kp[rows, :], vp[rows, :], lambda h: bn_ref[h])
            return carry
        lax.fori_loop(0, s_len // QBLK, unit, 0)

    def merge(d, first):
        sub = s_len // d
        step = min(sub, 256)
        for r in range(d):
            for c0 in range(0, sub, step):
                src = pl.ds(r * sub + c0, step)
                dst = residue_rows(d, r, c0, step)
                a, m, l = acc_p[src, :], m_p[src, :], l_p[src, :]
                if first:
                    a_run[dst, :] = a
                    m_run[dst, :] = m
                    l_run[dst, :] = l
                else:
                    m0 = m_run[dst, :]
                    mn = jnp.maximum(m0, m)
                    e0 = jnp.exp(m0 - mn)
                    e1 = jnp.exp(m - mn)
                    a_run[dst, :] = a_run[dst, :] * e0 + a * e1
                    l_run[dst, :] = l_run[dst, :] * e0 + l * e1
                    m_run[dst, :] = mn

    for pi, (window, d) in enumerate(DILATED_PATTERNS):
        sub = s_len // d
        permute(d)
        if sub >= KWIN:
            attend_wide(pi, d)
        else:
            attend_narrow()
        merge(d, pi == 0)

    def finish(i, carry):
        rows = pl.ds(pl.multiple_of(i * 256, 256), 256)
        o_ref[rows, :] = a_run[rows, :] / l_run[rows, :]
        return carry
    lax.fori_loop(0, s_len // 256, finish, 0)


def _attention(qkv, gq, gk, bias_wide, bias_narrow, batch, s_len):
    m = qkv.shape[0]
    pairs = ATTN_WIDTH // LANES
    slab = lambda off: pl.BlockSpec((s_len, LANES), lambda b, p: (b, off + p))
    row = pl.BlockSpec((1, LANES), lambda b, p: (0, 0))
    f32_slab = pltpu.VMEM((s_len, LANES), F32)
    bf16_slab = pltpu.VMEM((s_len, LANES), BF16)
    return pl.pallas_call(
        _attn_body,
        grid=(batch, pairs),
        in_specs=[
            slab(0), slab(pairs), slab(2 * pairs), row, row,
            pl.BlockSpec((2, 3, 2, QBLK, KWIN), lambda b, p: (0, 0, p, 0, 0)),
            pl.BlockSpec((2, QBLK, QBLK), lambda b, p: (p, 0, 0)),
        ],
        out_specs=pl.BlockSpec((s_len, LANES), lambda b, p: (b, p)),
        out_shape=jax.ShapeDtypeStruct((m, ATTN_WIDTH), F32),
        scratch_shapes=[f32_slab, f32_slab, bf16_slab, bf16_slab, bf16_slab, bf16_slab,
                        f32_slab, f32_slab, f32_slab, f32_slab, f32_slab, f32_slab],
        compiler_params=pltpu.CompilerParams(
            dimension_semantics=("parallel", "parallel"), vmem_limit_bytes=VMEM_LIMIT_BYTES),
        name="dilated_attention",
    )(qkv, qkv, qkv, gq, gk, bias_wide, bias_narrow)


def _t5_bucket(rel):
    nb = REL_BUCKETS // 2
    max_exact = nb // 2
    ret = jnp.where(rel > 0, nb, 0)
    n = jnp.abs(rel)
    nf = jnp.maximum(n, 1).astype(F32)
    large = max_exact + (jnp.log(nf / max_exact) / math.log(REL_MAX_DISTANCE / max_exact)
                         * (nb - max_exact)).astype(jnp.int32)
    large = jnp.minimum(large, nb - 1)
    return ret + jnp.where(n < max_exact, n, large)


def _bias_tables(rel_table):
    def table(d, n_keys, offset):
        i = jnp.arange(QBLK)[:, None]
        j = jnp.arange(n_keys)[None, :]
        delta = j - i - offset
        b = rel_table[_t5_bucket(delta * d)].astype(F32)
        b = jnp.where((jnp.abs(delta) <= BAND)[..., None], b, NEG_INF)
        return b.transpose(2, 0, 1)
    wide = jnp.stack([jnp.stack([table(d, KWIN, off) for off in (0, BAND, 2 * BAND)])
                      for (_, d) in DILATED_PATTERNS[:2]])
    narrow = table(DILATED_PATTERNS[2][1], QBLK, 0)
    return wide, narrow


def _ssd_body(x0_ref, x1_ref, x2_ref, x3_ref, b_ref, c_ref, z_ref, dt_ref,
              cwx_ref, cwb_ref, cwc_ref, cbx_ref, cbb_ref, cbc_ref,
              dtb_ref, alog_ref, dsk_ref, ng_ref, e_ref, o_ref,
              edge, xs_c, b_c, c_c, dt_s, sb_in, s_f, s_b, cum_t, cumx_t, tot_t, dt_t):
    x_refs = (x0_ref, x1_ref, x2_ref, x3_ref)
    s_len = z_ref.shape[0]
    t_len = SSD_CHUNK
    n_chunks = s_len // t_len
    lane = lax.broadcasted_iota(jnp.int32, (1, LANES), 1)
    head0 = lane < SSD_HEADDIM
    fwd_lanes = lane < GROUP_HEADS
    rr = lax.broadcasted_iota(jnp.int32, (t_len, t_len), 0)
    cc = lax.broadcasted_iota(jnp.int32, (t_len, t_len), 1)
    one = lambda mask: jnp.where(mask, 1.0, 0.0).astype(BF16)
    tri_le, tri_ge, tri_lt, tri_gt = one(cc <= rr), one(cc >= rr), one(cc < rr), one(cc > rr)
    ones_tt = jnp.ones((t_len, t_len), BF16)
    causal = cc <= rr
    anticausal = cc >= rr
    a_row = -jnp.exp(alog_ref[...])

    def conv_silu(src, w_ref, wcol, bias_ref, dst, dcol):
        def taps(load):
            acc = load(0) * w_ref[0:1, wcol:wcol + LANES]
            for k in range(1, SSD_CONV):
                acc = acc + load(k) * w_ref[k:k + 1, wcol:wcol + LANES]
            return _silu(acc + bias_ref[:, wcol:wcol + LANES])
        pad = SSD_CONV // 2
        lead = HALO - pad
        span = t_len + HALO
        edge[0:HALO, :] = jnp.zeros((HALO, LANES), F32)
        edge[HALO:HALO + span, :] = src[0:span, :]
        dst[0:t_len, dcol:dcol + LANES] = taps(lambda k: edge[lead + k:lead + k + t_len, :])

        def body(c, carry):
            base = c * t_len
            out = taps(lambda k: src[pl.ds(base + k - pad, t_len, stride=1), :])
            dst[pl.ds(pl.multiple_of(base, t_len), t_len), dcol:dcol + LANES] = out
            return carry
        lax.fori_loop(1, n_chunks - 1, body, 0)
        edge[0:span, :] = src[s_len - span:s_len, :]
        edge[span:span + HALO, :] = jnp.zeros((HALO, LANES), F32)
        dst[s_len - t_len:s_len, dcol:dcol + LANES] = taps(lambda k: edge[lead + k:lead + k + t_len, :])

    for pp in range(GROUP_PAIRS):
        conv_silu(x_refs[pp], cwx_ref, pp * LANES, cbx_ref, xs_c, pp * LANES)
    conv_silu(b_ref, cwb_ref, 0, cbb_ref, b_c, 0)
    conv_silu(c_ref, cwc_ref, 0, cbc_ref, c_c, 0)

    def softplus_rows(i, carry):
        rows = pl.ds(pl.multiple_of(i * 256, 256), 256)
        v = dt_ref[rows, :] + dtb_ref[...]
        dt_s[rows, :] = jnp.maximum(v, 0.0) + jnp.log1p(jnp.exp(-jnp.abs(v)))
        return carry
    lax.fori_loop(0, s_len // 256, softplus_rows, 0)

    def prep(rows):
        dt_c = dt_s[rows, :]
        parts = _split3(dt_c * a_row)
        cum = lambda tri: _dot(tri, parts[0]) + _dot(tri, parts[1]) + _dot(tri, parts[2])
        cum_incl = jnp.where(fwd_lanes, cum(tri_le), cum(tri_ge))
        cum_excl = jnp.where(fwd_lanes, cum(tri_gt), cum(tri_lt))
        cum_t[...] = cum_incl.T
        cumx_t[...] = cum_excl.T
        tot_t[...] = cum(ones_tt).T
        dt_t[...] = dt_c.T
        return cum_incl

    def state_weights(j):
        return jnp.exp(cumx_t[j:j + 1, :]) * dt_t[j:j + 1, :]

    def chunk_decay(pp, base):
        return jnp.where(head0, jnp.exp(tot_t[base + 2 * pp:base + 2 * pp + 1, :]),
                         jnp.exp(tot_t[base + 2 * pp + 1:base + 2 * pp + 2, :]))

    s_b[...] = jnp.zeros_like(s_b)
    s_f[...] = jnp.zeros_like(s_f)

    def bwd_states(i, carry):
        c = n_chunks - 1 - i
        rows = pl.ds(pl.multiple_of(c * t_len, t_len), t_len)
        prep(rows)
        b_t = b_c[rows, :].T
        for pp in range(GROUP_PAIRS):
            xsp = xs_c[rows, pp * LANES:(pp + 1) * LANES].astype(BF16)
            st = [_dot((b_t * state_weights(GROUP_HEADS + 2 * pp + hh)).astype(BF16), xsp) for hh in range(2)]
            sb_in[c, pp] = s_b[pp]
            s_b[pp] = s_b[pp] * chunk_decay(pp, GROUP_HEADS) + jnp.where(head0, st[0], st[1])
        return carry
    lax.fori_loop(0, n_chunks, bwd_states, 0)

    def outputs(c, carry):
        rows = pl.ds(pl.multiple_of(c * t_len, t_len), t_len)
        cum_parts = _split3(prep(rows))
        b_f = b_c[rows, :]
        b_t = b_f.T
        c_b = c_c[rows, :].astype(BF16)
        cb = _dot_nt(c_b, b_f.astype(BF16))
        sumsq = jnp.zeros((t_len, 1), F32)
        for pp in range(GROUP_PAIRS):
            cols = slice(pp * LANES, (pp + 1) * LANES)
            xs_f = xs_c[rows, cols]
            xsp = xs_f.astype(BF16)
            y_diag, st_f, col_f, col_b = [], [], [], []
            for hh in range(2):
                jf = 2 * pp + hh
                jb = GROUP_HEADS + jf
                colbc = lambda j: (_dot(cum_parts[0], e_ref[j]) + _dot(cum_parts[1], e_ref[j])
                                   + _dot(cum_parts[2], e_ref[j]))
                cf, cbk = colbc(jf), colbc(jb)
                dec_f = jnp.exp(jnp.where(causal, cf - cum_t[jf:jf + 1, :], NEG_INF))
                dec_b = jnp.exp(jnp.where(anticausal, cbk - cum_t[jb:jb + 1, :], NEG_INF))
                mix = cb * (dec_f * dt_t[jf:jf + 1, :] + dec_b * dt_t[jb:jb + 1, :])
                y_diag.append(_dot(mix.astype(BF16), xsp))
                st_f.append(_dot((b_t * state_weights(jf)).astype(BF16), xsp))
                col_f.append(cf)
                col_b.append(cbk)
            y = jnp.where(head0, y_diag[0], y_diag[1])
            y = y + _dot(c_b, s_f[pp].astype(BF16)) * jnp.exp(jnp.where(head0, col_f[0], col_f[1]))
            y = y + _dot(c_b, sb_in[c, pp].astype(BF16)) * jnp.exp(jnp.where(head0, col_b[0], col_b[1]))
            y = y + dsk_ref[:, cols] * xs_f
            s_f[pp] = s_f[pp] * chunk_decay(pp, 0) + jnp.where(head0, st_f[0], st_f[1])
            yz = y * _silu(z_ref[rows, cols])
            sumsq = sumsq + jnp.sum(yz * yz, axis=-1, keepdims=True)
            o_ref[rows, cols] = yz
        scale = lax.rsqrt(sumsq * (1.0 / GROUP_WIDTH) + EPS)
        o_ref[rows, :] = o_ref[rows, :] * scale * ng_ref[...]
        return carry
    lax.fori_loop(0, n_chunks, outputs, 0)


def _lane_select_matrices():
    e = np.zeros((2 * GROUP_HEADS, LANES, LANES), np.float32)
    for j in range(2 * GROUP_HEADS):
        e[j, j, :] = 1.0
    return jnp.asarray(e, BF16)


def _ssd(xbc, z, dt, conv_w, conv_b, dt_bias, a_log, d_lanes, norm_g, batch, s_len):
    m = xbc.shape[0]
    g_of = lambda b, g: (b, g)
    xslab = lambda pp: pl.BlockSpec((s_len, LANES), lambda b, g: (b, g * GROUP_PAIRS + pp))
    b_off = SSD_INNER // LANES
    c_off = b_off + SSD_GROUPS
    n_chunks = s_len // SSD_CHUNK
    state = (GROUP_PAIRS, SSD_STATE, LANES)
    return pl.pallas_call(
        _ssd_body,
        grid=(batch, SSD_GROUPS),
        in_specs=[
            xslab(0), xslab(1), xslab(2), xslab(3),
            pl.BlockSpec((s_len, LANES), lambda b, g: (b, b_off + g)),
            pl.BlockSpec((s_len, LANES), lambda b, g: (b, c_off + g)),
            pl.BlockSpec((s_len, GROUP_WIDTH), g_of),
            pl.BlockSpec((s_len, LANES), g_of),
            pl.BlockSpec((SSD_CONV, GROUP_WIDTH), lambda b, g: (0, g)),
            pl.BlockSpec((SSD_CONV, LANES), lambda b, g: (0, b_off + g)),
            pl.BlockSpec((SSD_CONV, LANES), lambda b, g: (0, c_off + g)),
            pl.BlockSpec((1, GROUP_WIDTH), lambda b, g: (0, g)),
            pl.BlockSpec((1, LANES), lambda b, g: (0, b_off + g)),
            pl.BlockSpec((1, LANES), lambda b, g: (0, c_off + g)),
            pl.BlockSpec((1, LANES), lambda b, g: (0, g)),
            pl.BlockSpec((1, LANES), lambda b, g: (0, g)),
            pl.BlockSpec((1, GROUP_WIDTH), lambda b, g: (0, g)),
            pl.BlockSpec((1, GROUP_WIDTH), lambda b, g: (0, g)),
            pl.BlockSpec((2 * GROUP_HEADS, LANES, LANES), lambda b, g: (0, 0, 0)),
        ],
        out_specs=pl.BlockSpec((s_len, GROUP_WIDTH), g_of),
        out_shape=jax.ShapeDtypeStruct((m, SSD_INNER), F32),
        scratch_shapes=[
            pltpu.VMEM((SSD_CHUNK + 2 * HALO, LANES), F32),
            pltpu.VMEM((s_len, GROUP_WIDTH), F32),
            pltpu.VMEM((s_len, LANES), F32),
            pltpu.VMEM((s_len, LANES), F32),
            pltpu.VMEM((s_len, LANES), F32),
            pltpu.VMEM((n_chunks,) + state, F32),
            pltpu.VMEM(state, F32),
            pltpu.VMEM(state, F32),
            pltpu.VMEM((LANES, SSD_CHUNK), F32),
            pltpu.VMEM((LANES, SSD_CHUNK), F32),
            pltpu.VMEM((LANES, SSD_CHUNK), F32),
            pltpu.VMEM((LANES, SSD_CHUNK), F32),
        ],
        compiler_params=pltpu.CompilerParams(
            dimension_semantics=("parallel", "parallel"), vmem_limit_bytes=VMEM_LIMIT_BYTES),
        name="ssd",
    )(xbc, xbc, xbc, xbc, xbc, xbc, z, dt, conv_w, conv_w, conv_w, conv_b, conv_b, conv_b,
      dt_bias, a_log, d_lanes, norm_g, _lane_select_matrices())


def _outproj_body(s_len, a_ref, s_ref, sc_ref, scp_ref, scn_ref, x_ref, w_ref, cw_ref, cb_ref, o_ref, cx):
    tm = x_ref.shape[0]
    blocks_per_seq = s_len // tm
    i = pl.program_id(0) % blocks_per_seq
    w = SC_WIDTH
    gate_cx = lambda ref, rows: ref[rows, w:2 * w] * ref[rows, 2 * w:3 * w]
    cx[HALO:HALO + tm, :] = gate_cx(sc_ref, slice(None))
    cx[HALO - 1:HALO, :] = jnp.where(i == 0, 0.0, gate_cx(scp_ref, slice(HALO - 1, HALO)))
    cx[HALO + tm:HALO + tm + 1, :] = jnp.where(i == blocks_per_seq - 1, 0.0, gate_cx(scn_ref, slice(0, 1)))
    conv = cb_ref[...]
    for k in range(3):
        conv = conv + cw_ref[k:k + 1, :] * cx[HALO - 1 + k:HALO - 1 + k + tm, :]
    c_out = (sc_ref[:, 0:w] * conv).astype(BF16)
    acc = _dot(a_ref[...].astype(BF16), w_ref[0:ATTN_WIDTH, :])
    acc = acc + _dot(s_ref[...].astype(BF16), w_ref[ATTN_WIDTH:ATTN_WIDTH + SSD_INNER, :])
    acc = acc + _dot(c_out, w_ref[ATTN_WIDTH + SSD_INNER:, :])
    o_ref[...] = x_ref[...] + acc


def _out_proj(a, s, sc, x2, w, cw, cb, s_len):
    m = x2.shape[0]
    tm = OUT_TM
    halo_blocks = tm // HALO
    last_halo = m // HALO - 1
    rows = lambda n: pl.BlockSpec((tm, n), lambda i: (i, 0))
    return pl.pallas_call(
        functools.partial(_outproj_body, s_len),
        grid=(m // tm,),
        in_specs=[
            rows(ATTN_WIDTH), rows(SSD_INNER), rows(SC_W),
            pl.BlockSpec((HALO, SC_W), lambda i: (jnp.maximum(i * halo_blocks - 1, 0), 0)),
            pl.BlockSpec((HALO, SC_W), lambda i: (jnp.minimum((i + 1) * halo_blocks, last_halo), 0)),
            rows(D_MODEL),
            pl.BlockSpec(w.shape, lambda i: (0, 0), pipeline_mode=pl.Buffered(1)),
            pl.BlockSpec((3, SC_WIDTH), lambda i: (0, 0)),
            pl.BlockSpec((1, SC_WIDTH), lambda i: (0, 0)),
        ],
        out_specs=rows(D_MODEL),
        out_shape=jax.ShapeDtypeStruct((m, D_MODEL), F32),
        scratch_shapes=[pltpu.VMEM((tm + 2 * HALO, SC_WIDTH), F32)],
        compiler_params=pltpu.CompilerParams(
            dimension_semantics=("parallel",), vmem_limit_bytes=VMEM_LIMIT_BYTES),
        name="out_proj",
    )(a, s, sc, sc, sc, x2, w, cw, cb)


def _ffn_body(s_len, x_ref, xp_ref, xn_ref, g_ref, wu_ref, cw_ref, cb_ref, wd_ref, o_ref, h, ug, uv):
    tm = x_ref.shape[0]
    blocks_per_seq = s_len // tm
    i = pl.program_id(0) % blocks_per_seq

    def norm(x):
        ms = jnp.mean(x * x, axis=-1, keepdims=True)
        return x * lax.rsqrt(ms + EPS) * g_ref[...]

    h[0:tm, :] = norm(x_ref[...]).astype(BF16)
    halo = jnp.concatenate([jnp.where(i == 0, 0.0, norm(xp_ref[...])),
                            jnp.where(i == blocks_per_seq - 1, 0.0, norm(xn_ref[...]))], axis=0)
    h[tm:tm + 2 * HALO, :] = halo.astype(BF16)

    def conv3(u_scr, u, col):
        u_scr[HALO:HALO + tm, :] = u[0:tm]
        u_scr[0:HALO, :] = u[tm:tm + HALO]
        u_scr[HALO + tm:2 * HALO + tm, :] = u[tm + HALO:tm + 2 * HALO]
        out = cb_ref[:, col:col + FFN_TF]
        for k in range(3):
            out = out + cw_ref[k:k + 1, col:col + FFN_TF] * u_scr[HALO - 1 + k:HALO - 1 + k + tm, :]
        return out

    for j in range(D_FF // FFN_TF):
        gcol = j * FFN_TF
        vcol = D_FF + gcol
        hb = h[...]
        gate = conv3(ug, _dot(hb, wu_ref[:, gcol:gcol + FFN_TF]), gcol)
        val = conv3(uv, _dot(hb, wu_ref[:, vcol:vcol + FFN_TF]), vcol)
        act = (_silu(gate) * val).astype(BF16)
        down = _dot(act, wd_ref[gcol:gcol + FFN_TF, :])
        if j == 0:
            o_ref[...] = x_ref[...] + down
        else:
            o_ref[...] += down


def _ffn(x2, g, wu, cw, cb, wd, s_len):
    m = x2.shape[0]
    tm = FFN_TM
    halo_blocks = tm // HALO
    last_halo = m // HALO - 1
    const = lambda a: pl.BlockSpec(a.shape, lambda i: (0, 0), pipeline_mode=pl.Buffered(1))
    return pl.pallas_call(
        functools.partial(_ffn_body, s_len),
        grid=(m // tm,),
        in_specs=[
            pl.BlockSpec((tm, D_MODEL), lambda i: (i, 0)),
            pl.BlockSpec((HALO, D_MODEL), lambda i: (jnp.maximum(i * halo_blocks - 1, 0), 0)),
            pl.BlockSpec((HALO, D_MODEL), lambda i: (jnp.minimum((i + 1) * halo_blocks, last_halo), 0)),
            pl.BlockSpec((1, D_MODEL), lambda i: (0, 0)),
            const(wu), const(cw), const(cb), const(wd),
        ],
        out_specs=pl.BlockSpec((tm, D_MODEL), lambda i: (i, 0)),
        out_shape=jax.ShapeDtypeStruct((m, D_MODEL), F32),
        scratch_shapes=[
            pltpu.VMEM((tm + 2 * HALO, D_MODEL), BF16),
            pltpu.VMEM((tm + 2 * HALO, FFN_TF), F32),
            pltpu.VMEM((tm + 2 * HALO, FFN_TF), F32),
        ],
        compiler_params=pltpu.CompilerParams(
            dimension_semantics=("parallel",), vmem_limit_bytes=VMEM_LIMIT_BYTES),
        name="conv_ffn",
    )(x2, x2, x2, g, wu, cw, cb, wd)


def _regroup_dt(v):
    lead = v.shape[:-1]
    v = v.reshape(lead + (2, SSD_GROUPS, GROUP_HEADS))
    v = jnp.moveaxis(v, -3, -2).reshape(lead + (SSD_GROUPS, 2 * GROUP_HEADS))
    v = jnp.pad(v, [(0, 0)] * len(lead) + [(0, 0), (0, LANES - 2 * GROUP_HEADS)])
    return v.reshape(lead + (DT_W,))


def _prep_w_in(w):
    head = 3 * ATTN_WIDTH + SSD_INNER + SSD_XBC
    dt_cols = _regroup_dt(w[:, head:head + 2 * SSD_HEADS])
    return jnp.concatenate([w[:, :head], dt_cols, w[:, head + 2 * SSD_HEADS:]], axis=1).astype(BF16)


def kernel(x, rel_table, norm1_g, w_in, q_norm_g, k_norm_g, ssd_conv_w, ssd_conv_b, ssd_dt_bias, ssd_a_log, ssd_d,
           ssd_norm_g, sc_conv_w, sc_conv_b, w_out, norm2_g, ffn_w_up, ffn_conv_w, ffn_conv_b, ffn_w_down):
    batch, s_len, d_model = x.shape
    assert d_model == D_MODEL and s_len == DILATED_PATTERNS[-1][0]
    assert s_len % OUT_TM == 0 and s_len % FFN_TM == 0 and s_len % IN_TM == 0
    depth = w_in.shape[0]
    x2 = x.reshape(batch * s_len, d_model)
    bias_wide, bias_narrow = _bias_tables(rel_table)
    row = lambda v: v.reshape(1, -1).astype(F32)
    for i in range(depth):
        qkv, z, xbc, dt, sc = _in_proj(x2, row(norm1_g[i]), _prep_w_in(w_in[i]))
        gq = row(jnp.tile(q_norm_g[i], 2)) * (HEAD_DIM ** -0.5)
        gk = row(jnp.tile(k_norm_g[i], 2))
        a_out = _attention(qkv, gq, gk, bias_wide, bias_narrow, batch, s_len)
        s_out = _ssd(xbc, z, dt, ssd_conv_w[i], row(ssd_conv_b[i]),
                     row(_regroup_dt(ssd_dt_bias[i].reshape(-1))), row(_regroup_dt(ssd_a_log[i].reshape(-1))),
                     row(jnp.repeat(ssd_d[i], SSD_HEADDIM)), row(ssd_norm_g[i]), batch, s_len)
        x2 = _out_proj(a_out, s_out, sc, x2, w_out[i].astype(BF16), sc_conv_w[i], row(sc_conv_b[i]), s_len)
        x2 = _ffn(x2, row(norm2_g[i]), ffn_w_up[i].astype(BF16), ffn_conv_w[i], row(ffn_conv_b[i]),
                  ffn_w_down[i].astype(BF16), s_len)
    return x2.reshape(batch, s_len, d_model)
```

```python
import functools
import math

import jax
import jax.numpy as jnp
from jax import lax
from jax.experimental import pallas as pl
from jax.experimental.pallas import tpu as pltpu

F32 = jnp.float32
BF16 = jnp.bfloat16

LANES = 128
VMEM_LIMIT_BYTES = 56 * 1024 * 1024

D_MODEL = 1024
ATTN_HEADS = 8
HEAD_DIM = 64
ATTN_WIDTH = ATTN_HEADS * HEAD_DIM
DILATED_PATTERNS = ((128, 1), (512, 4), (2048, 16))
BAND = 64
REL_BUCKETS = 32
REL_MAX_DISTANCE = 1024
SSD_HEADS = 16
SSD_HEADDIM = 64
SSD_INNER = SSD_HEADS * SSD_HEADDIM
SSD_GROUPS = 2
SSD_STATE = 128
SSD_XBC = SSD_INNER + 2 * SSD_GROUPS * SSD_STATE
SSD_CONV = 7
SSD_CHUNK = 128
GROUP_HEADS = SSD_HEADS // SSD_GROUPS
GROUP_WIDTH = SSD_INNER // SSD_GROUPS
GROUP_PAIRS = GROUP_WIDTH // LANES
SC_WIDTH = 512
D_FF = 2816
EPS = 1e-6
NEG_INF = -1e30
LOG2_E = math.log2(math.e)

QKV_W = 3 * ATTN_WIDTH
DT_W = SSD_GROUPS * LANES
SC_W = 3 * SC_WIDTH
IN_PIECES = (QKV_W, SSD_INNER, SSD_XBC, DT_W, SC_W)

IN_TM = 256
OUT_TM = 512
FFN_TM = 512
FFN_TILES = (768, 768, 768, 512)
assert sum(FFN_TILES) == D_FF
HALO = 8
QBLK = 128
KWIN = 2 * QBLK
ATTN_UNROLL = 8
SSD_UNROLL = 2
PREP_UNROLL = 4
DT_SLOTS = 2 * GROUP_HEADS
DT_COPIES = 4
ROW_SRC, ROW_END, ROW_DT, ROW_TOT = (q * DT_SLOTS for q in range(DT_COPIES))


def _split3(x):
    p0 = x.astype(BF16)
    r = x - p0.astype(F32)
    p1 = r.astype(BF16)
    p2 = (r - p1.astype(F32)).astype(BF16)
    return p0, p1, p2


def _dot(a, b):
    return jnp.dot(a, b, preferred_element_type=F32)


def _dot_nt(a, b):
    return lax.dot_general(a, b, (((1,), (1,)), ((), ())), preferred_element_type=F32)


def _silu(x):
    return x / (1.0 + jnp.exp(-x))


def _inproj_body(x_ref, g_ref, w_ref, qkv_ref, z_ref, xbc_ref, dt_ref, sc_ref):
    x = x_ref[...]
    ms = jnp.mean(x * x, axis=-1, keepdims=True)
    hb = (x * lax.rsqrt(ms + EPS) * g_ref[...]).astype(BF16)
    off = 0
    for ref in (qkv_ref, z_ref, xbc_ref, dt_ref, sc_ref):
        n = ref.shape[-1]
        ref[...] = _dot(hb, w_ref[:, off:off + n])
        off += n


def _in_proj(x2, g, w):
    m = x2.shape[0]
    n_total = sum(IN_PIECES)
    return pl.pallas_call(
        _inproj_body,
        grid=(m // IN_TM,),
        in_specs=[
            pl.BlockSpec((IN_TM, D_MODEL), lambda i: (i, 0)),
            pl.BlockSpec((1, D_MODEL), lambda i: (0, 0)),
            pl.BlockSpec((D_MODEL, n_total), lambda i: (0, 0), pipeline_mode=pl.Buffered(1)),
        ],
        out_specs=[pl.BlockSpec((IN_TM, n), lambda i: (i, 0)) for n in IN_PIECES],
        out_shape=[jax.ShapeDtypeStruct((m, n), F32) for n in IN_PIECES],
        compiler_params=pltpu.CompilerParams(
            dimension_semantics=("parallel",), vmem_limit_bytes=VMEM_LIMIT_BYTES),
        name="in_proj",
    )(x2, g, w)


def _attn_body(q_ref, k_ref, v_ref, gq_ref, gk_ref, bw_ref, bn_ref, o_ref,
               qn, kn, q4, k4, v4, qp0, qp1, kp, vp, res_a, res_b):
    s_len = q_ref.shape[0]
    (_, d_fine), (_, d_mid), (_, d_coarse) = DILATED_PATTERNS
    ratio = d_mid // d_fine
    assert d_fine == 1 and d_coarse == ratio * d_mid
    sub_mid = s_len // d_mid
    sub_coarse = s_len // d_coarse
    assert sub_coarse == QBLK
    lane = lax.broadcasted_iota(jnp.int32, (1, LANES), 1)
    head0 = lane < HEAD_DIM
    rr = lax.broadcasted_iota(jnp.int32, (LANES, LANES), 0)
    cc = lax.broadcasted_iota(jnp.int32, (LANES, LANES), 1)
    same_head = jnp.where((rr < HEAD_DIM) == (cc < HEAD_DIM), 1.0, 0.0).astype(BF16)
    chunk = 256

    for c0 in range(0, s_len, chunk):
        rows = pl.ds(c0, chunk)
        for src, g_ref, dst in ((q_ref, gq_ref, qn), (k_ref, gk_ref, kn)):
            x = src[rows, :]
            sq = x * x
            hi = sq.astype(BF16)
            lo = (sq - hi.astype(F32)).astype(BF16)
            ss = _dot(hi, same_head) + _dot(lo, same_head)
            dst[rows, :] = x * lax.rsqrt(ss * (1.0 / HEAD_DIM) + EPS) * g_ref[...]

    for r in range(ratio):
        for c0 in range(0, sub_mid, chunk):
            src = pl.ds(r + c0 * ratio, chunk, stride=ratio)
            dst = pl.ds(r * sub_mid + c0, chunk)
            q4[dst, :] = qn[src, :]
            k4[dst, :] = kn[src, :]
            v4[dst, :] = v_ref[src, :]

    def write_operands(dst, qv, kv, vv):
        qp0[dst, :] = jnp.where(head0, qv, 0.0).astype(BF16)
        qp1[dst, :] = jnp.where(head0, 0.0, qv).astype(BF16)
        kp[dst, :] = kv.astype(BF16)
        vp[dst, :] = vv.astype(BF16)

    def softmax_units(res, specs):
        scores = []
        for qrows, krows, bias_of_head in specs:
            kw = kp[krows, :]
            for h, qp in enumerate((qp0, qp1)):
                scores.append(_dot_nt(qp[qrows, :], kw) + bias_of_head(h))
        probs = []
        for s in scores:
            m = jnp.max(s, axis=-1, keepdims=True)
            p = jnp.exp(s - m)
            probs.append((p.astype(BF16), m, jnp.sum(p, axis=-1, keepdims=True)))
        for i, (qrows, krows, _) in enumerate(specs):
            vw = vp[krows, :]
            (p0, m0, l0), (p1, m1, l1) = probs[2 * i], probs[2 * i + 1]
            res[0, qrows, :] = jnp.where(head0, _dot(p0, vw), _dot(p1, vw))
            res[1, qrows, :] = jnp.where(head0, m0, m1)
            res[2, qrows, :] = jnp.where(head0, l0, l1)

    def attend_wide(res, pi, sub):
        nb = sub // QBLK

        def units(i, carry):
            specs = []
            for j in range(ATTN_UNROLL):
                u = i * ATTN_UNROLL + j
                t = u % nb
                qrows = pl.ds(pl.multiple_of(u * QBLK, QBLK), QBLK)
                kstart = (u - t) * QBLK + jnp.clip(t * QBLK - BAND, 0, sub - KWIN)
                krows = pl.ds(pl.multiple_of(kstart, BAND), KWIN)
                var = jnp.where(t == 0, 0, jnp.where(t == nb - 1, 2, 1))
                specs.append((qrows, krows, lambda h, var=var: bw_ref[pi, var, h]))
            softmax_units(res, specs)
            return carry
        lax.fori_loop(0, s_len // QBLK // ATTN_UNROLL, units, 0)

    def attend_narrow(res):
        def units(i, carry):
            specs = []
            for j in range(ATTN_UNROLL):
                rows = pl.ds(pl.multiple_of((i * ATTN_UNROLL + j) * QBLK, QBLK), QBLK)
                specs.append((rows, rows, lambda h: bn_ref[h]))
            softmax_units(res, specs)
            return carry
        lax.fori_loop(0, s_len // QBLK // ATTN_UNROLL, units, 0)

    def merged(a0, m0, l0, a1, m1, l1):
        mn = jnp.maximum(m0, m1)
        e0 = jnp.exp(m0 - mn)
        e1 = jnp.exp(m1 - mn)
        return a0 * e0 + a1 * e1, mn, l0 * e0 + l1 * e1

    for r in range(ratio):
        for r2 in range(ratio):
            src = pl.ds(r * sub_mid + r2, sub_coarse, stride=ratio)
            dst = pl.ds((r + ratio * r2) * sub_coarse, sub_coarse)
            write_operands(dst, q4[src, :], k4[src, :], v4[src, :])
    attend_narrow(res_a)

    for c0 in range(0, s_len, chunk):
        rows = pl.ds(c0, chunk)
        write_operands(rows, q4[rows, :], k4[rows, :], v4[rows, :])
    attend_wide(res_b, 1, sub_mid)
    for r in range(ratio):
        for r2 in range(ratio):
            fine = pl.ds(r * sub_mid + r2, sub_coarse, stride=ratio)
            coarse = pl.ds((r + ratio * r2) * sub_coarse, sub_coarse)
            a, m, l = merged(res_b[0, fine, :], res_b[1, fine, :], res_b[2, fine, :],
                             res_a[0, coarse, :], res_a[1, coarse, :], res_a[2, coarse, :])
            res_b[0, fine, :] = a
            res_b[1, fine, :] = m
            res_b[2, fine, :] = l

    for c0 in range(0, s_len, chunk):
        rows = pl.ds(c0, chunk)
        write_operands(rows, qn[rows, :], kn[rows, :], v_ref[rows, :])
    attend_wide(res_a, 0, s_len)
    for r in range(ratio):
        for c0 in range(0, sub_mid, chunk):
            nat = pl.ds(r + c0 * ratio, chunk, stride=ratio)
            mid = pl.ds(r * sub_mid + c0, chunk)
            a, _, l = merged(res_a[0, nat, :], res_a[1, nat, :], res_a[2, nat, :],
                             res_b[0, mid, :], res_b[1, mid, :], res_b[2, mid, :])
            o_ref[nat, :] = a / l


def _attention(qkv, gq, gk, bias_wide, bias_narrow, batch, s_len):
    m = qkv.shape[0]
    pairs = ATTN_WIDTH // LANES
    slab = lambda off: pl.BlockSpec((s_len, LANES), lambda b, p: (b, off + p))
    row = pl.BlockSpec((1, LANES), lambda b, p: (0, 0))
    f32_slab = pltpu.VMEM((s_len, LANES), F32)
    bf16_slab = pltpu.VMEM((s_len, LANES), BF16)
    stats = pltpu.VMEM((3, s_len, LANES), F32)
    return pl.pallas_call(
        _attn_body,
        grid=(batch, pairs),
        in_specs=[
            slab(0), slab(pairs), slab(2 * pairs), row, row,
            pl.BlockSpec((2, 3, 2, QBLK, KWIN), lambda b, p: (0, 0, p, 0, 0)),
            pl.BlockSpec((2, QBLK, QBLK), lambda b, p: (p, 0, 0)),
        ],
        out_specs=pl.BlockSpec((s_len, LANES), lambda b, p: (b, p)),
        out_shape=jax.ShapeDtypeStruct((m, ATTN_WIDTH), F32),
        scratch_shapes=[f32_slab, f32_slab, f32_slab, f32_slab, f32_slab,
                        bf16_slab, bf16_slab, bf16_slab, bf16_slab, stats, stats],
        compiler_params=pltpu.CompilerParams(
            dimension_semantics=("parallel", "parallel"), vmem_limit_bytes=VMEM_LIMIT_BYTES),
        name="dilated_attention",
    )(qkv, qkv, qkv, gq, gk, bias_wide, bias_narrow)


def _t5_bucket(rel):
    nb = REL_BUCKETS // 2
    max_exact = nb // 2
    ret = jnp.where(rel > 0, nb, 0)
    n = jnp.abs(rel)
    nf = jnp.maximum(n, 1).astype(F32)
    large = max_exact + (jnp.log(nf / max_exact) / math.log(REL_MAX_DISTANCE / max_exact)
                         * (nb - max_exact)).astype(jnp.int32)
    large = jnp.minimum(large, nb - 1)
    return ret + jnp.where(n < max_exact, n, large)


def _bias_tables(rel_table):
    def table(d, n_keys, offset):
        i = jnp.arange(QBLK)[:, None]
        j = jnp.arange(n_keys)[None, :]
        delta = j - i - offset
        b = rel_table[_t5_bucket(delta * d)].astype(F32)
        b = jnp.where((jnp.abs(delta) <= BAND)[..., None], b, NEG_INF)
        return b.transpose(2, 0, 1)
    wide = jnp.stack([jnp.stack([table(d, KWIN, off) for off in (0, BAND, 2 * BAND)])
                      for (_, d) in DILATED_PATTERNS[:2]])
    narrow = table(DILATED_PATTERNS[2][1], QBLK, 0)
    return wide, narrow


def _ssd_body(x0_ref, x1_ref, x2_ref, x3_ref, b_ref, c_ref, z_ref, dt_ref,
              cwx_ref, cwb_ref, cwc_ref, cbx_ref, cbb_ref, cbc_ref,
              dtb_ref, alog_ref, dsk_ref, ng_ref, o_ref,
              edge, xs_c, b_c, c_c, sb_in, s_f, s_b, col, rows_t, bt_s):
    x_refs = (x0_ref, x1_ref, x2_ref, x3_ref)
    s_len = z_ref.shape[0]
    t_len = SSD_CHUNK
    n_chunks = s_len // t_len
    lane = lax.broadcasted_iota(jnp.int32, (1, LANES), 1)
    head0 = lane < SSD_HEADDIM
    fwd_slot = lane % DT_SLOTS < GROUP_HEADS
    copy = lane // DT_SLOTS
    rr = lax.broadcasted_iota(jnp.int32, (t_len, t_len), 0)
    cc = lax.broadcasted_iota(jnp.int32, (t_len, t_len), 1)
    tri_le = jnp.where(cc <= rr, 1.0, 0.0).astype(BF16)
    causal = cc <= rr
    diag = cc == rr
    a_row = -jnp.exp(alog_ref[...]) * LOG2_E

    def conv_silu(src, w_ref, wcol, bias_ref, dst, dcol):
        def taps(load):
            acc = load(0) * w_ref[0:1, wcol:wcol + LANES]
            for k in range(1, SSD_CONV):
                acc = acc + load(k) * w_ref[k:k + 1, wcol:wcol + LANES]
            return _silu(acc + bias_ref[:, wcol:wcol + LANES])
        pad = SSD_CONV // 2
        lead = HALO - pad
        span = t_len + HALO
        edge[0:HALO, :] = jnp.zeros((HALO, LANES), F32)
        edge[HALO:HALO + span, :] = src[0:span, :]
        dst[0:t_len, dcol:dcol + LANES] = taps(lambda k: edge[lead + k:lead + k + t_len, :])

        def body(c, carry):
            base = c * t_len
            out = taps(lambda k: src[pl.ds(base + k - pad, t_len, stride=1), :])
            dst[pl.ds(pl.multiple_of(base, t_len), t_len), dcol:dcol + LANES] = out
            return carry
        lax.fori_loop(1, n_chunks - 1, body, 0)
        edge[0:span, :] = src[s_len - span:s_len, :]
        edge[span:span + HALO, :] = jnp.zeros((HALO, LANES), F32)
        dst[s_len - t_len:s_len, dcol:dcol + LANES] = taps(lambda k: edge[lead + k:lead + k + t_len, :])

    for pp in range(GROUP_PAIRS):
        conv_silu(x_refs[pp], cwx_ref, pp * LANES, cbx_ref, xs_c, pp * LANES)
    conv_silu(b_ref, cwb_ref, 0, cbb_ref, b_c, 0)
    conv_silu(c_ref, cwc_ref, 0, cbc_ref, c_c, 0)

    chunk_rows = lambda c: pl.ds(pl.multiple_of(c * t_len, t_len), t_len)
    pair_cols = lambda pp: slice(pp * LANES, (pp + 1) * LANES)

    def prepare(i, carry):
        for k in range(PREP_UNROLL):
            c = i * PREP_UNROLL + k
            rows = chunk_rows(c)
            v = dt_ref[rows, :] + dtb_ref[...]
            dt_c = jnp.maximum(v, 0.0) + jnp.log1p(jnp.exp(-jnp.abs(v)))
            dta = dt_c * a_row
            p0, p1, p2 = _split3(dta)
            fwd_incl = _dot(tri_le, p0) + _dot(tri_le, p1) + _dot(tri_le, p2)
            total = fwd_incl[t_len - 1:t_len, :]
            incl = jnp.where(fwd_slot, fwd_incl, total - fwd_incl + dta)
            to_edge = jnp.where(fwd_slot, total - fwd_incl, fwd_incl - dta)
            col[c] = incl
            packed = jnp.where(copy == 0, incl - jnp.log2(dt_c),
                               jnp.where(copy == 1, to_edge, jnp.where(copy == 2, dt_c, total)))
            rows_t[c] = packed.T
            bt_s[c] = b_c[rows, :].T
        return carry
    lax.fori_loop(0, n_chunks // PREP_UNROLL, prepare, 0)

    def state_lhs(c, j):
        w = jnp.exp2(rows_t[c, ROW_END + j:ROW_END + j + 1, :]) * rows_t[c, ROW_DT + j:ROW_DT + j + 1, :]
        return (bt_s[c] * w).astype(BF16)

    def chunk_decay(c, pp, base):
        j = ROW_TOT + base + 2 * pp
        return jnp.where(head0, jnp.exp2(rows_t[c, j:j + 1, :]), jnp.exp2(rows_t[c, j + 1:j + 2, :]))

    s_b[...] = jnp.zeros_like(s_b)
    s_f[...] = jnp.zeros_like(s_f)

    def bwd_states(i, carry):
        chunks = [n_chunks - 1 - (i * SSD_UNROLL + k) for k in range(SSD_UNROLL)]
        local = []
        for c in chunks:
            rows = chunk_rows(c)
            lhs = [state_lhs(c, GROUP_HEADS + e) for e in range(GROUP_HEADS)]
            st = []
            for pp in range(GROUP_PAIRS):
                xsp = xs_c[rows, pair_cols(pp)].astype(BF16)
                st.append(jnp.where(head0, _dot(lhs[2 * pp], xsp), _dot(lhs[2 * pp + 1], xsp)))
            local.append(st)
        for k, c in enumerate(chunks):
            for pp in range(GROUP_PAIRS):
                sb_in[c, pp] = s_b[pp]
                s_b[pp] = s_b[pp] * chunk_decay(c, pp, GROUP_HEADS) + local[k][pp]
        return carry
    lax.fori_loop(0, n_chunks // SSD_UNROLL, bwd_states, 0)

    def outputs(i, carry):
        chunks = [i * SSD_UNROLL + k for k in range(SSD_UNROLL)]
        staged = []
        for c in chunks:
            rows = chunk_rows(c)
            c_b = c_c[rows, :].astype(BF16)
            cb = _dot(c_b, bt_s[c].astype(BF16))
            incl = col[c]
            lane_bcast = lambda j: jnp.broadcast_to(incl[:, j:j + 1], (t_len, t_len))
            src_row = lambda j: rows_t[c, ROW_SRC + j:ROW_SRC + j + 1, :]
            mix, col_f, col_b = [], [], []
            for e in range(GROUP_HEADS):
                jf, jb = e, GROUP_HEADS + e
                cf, cbk = lane_bcast(jf), lane_bcast(jb)
                w = jnp.exp2(jnp.where(causal, cf - src_row(jf), cbk - src_row(jb)))
                w = w + jnp.where(diag, rows_t[c, ROW_DT + jb:ROW_DT + jb + 1, :], 0.0)
                mix.append((cb * w).astype(BF16))
                col_f.append(cf)
                col_b.append(cbk)
            lhs_f = [state_lhs(c, e) for e in range(GROUP_HEADS)]
            per_pair = []
            for pp in range(GROUP_PAIRS):
                xs_f = xs_c[rows, pair_cols(pp)]
                xsp = xs_f.astype(BF16)
                y = jnp.where(head0, _dot(mix[2 * pp], xsp), _dot(mix[2 * pp + 1], xsp))
                y = y + (_dot(c_b, sb_in[c, pp].astype(BF16))
                         * jnp.exp2(jnp.where(head0, col_b[2 * pp], col_b[2 * pp + 1])))
                y = y + dsk_ref[:, pair_cols(pp)] * xs_f
                st = jnp.where(head0, _dot(lhs_f[2 * pp], xsp), _dot(lhs_f[2 * pp + 1], xsp))
                scale_f = jnp.exp2(jnp.where(head0, col_f[2 * pp], col_f[2 * pp + 1]))
                per_pair.append((y, st, scale_f))
            staged.append((c, rows, c_b, per_pair))
        for c, rows, c_b, per_pair in staged:
            sq = jnp.zeros((t_len, LANES), F32)
            for pp, (y, st, scale_f) in enumerate(per_pair):
                y = y + _dot(c_b, s_f[pp].astype(BF16)) * scale_f
                s_f[pp] = s_f[pp] * chunk_decay(c, pp, 0) + st
                yz = y * _silu(z_ref[rows, pair_cols(pp)])
                sq = sq + yz * yz
                o_ref[rows, pair_cols(pp)] = yz
            scale = lax.rsqrt(jnp.sum(sq, axis=-1, keepdims=True) * (1.0 / GROUP_WIDTH) + EPS)
            o_ref[rows, :] = o_ref[rows, :] * scale * ng_ref[...]
        return carry
    lax.fori_loop(0, n_chunks // SSD_UNROLL, outputs, 0)


def _ssd(xbc, z, dt, conv_w, conv_b, dt_bias, a_log, d_lanes, norm_g, batch, s_len):
    m = xbc.shape[0]
    g_of = lambda b, g: (b, g)
    xslab = lambda pp: pl.BlockSpec((s_len, LANES), lambda b, g: (b, g * GROUP_PAIRS + pp))
    b_off = SSD_INNER // LANES
    c_off = b_off + SSD_GROUPS
    n_chunks = s_len // SSD_CHUNK
    state = (GROUP_PAIRS, SSD_STATE, LANES)
    return pl.pallas_call(
        _ssd_body,
        grid=(batch, SSD_GROUPS),
        in_specs=[
            xslab(0), xslab(1), xslab(2), xslab(3),
            pl.BlockSpec((s_len, LANES), lambda b, g: (b, b_off + g)),
            pl.BlockSpec((s_len, LANES), lambda b, g: (b, c_off + g)),
            pl.BlockSpec((s_len, GROUP_WIDTH), g_of),
            pl.BlockSpec((s_len, LANES), g_of),
            pl.BlockSpec((SSD_CONV, GROUP_WIDTH), lambda b, g: (0, g)),
            pl.BlockSpec((SSD_CONV, LANES), lambda b, g: (0, b_off + g)),
            pl.BlockSpec((SSD_CONV, LANES), lambda b, g: (0, c_off + g)),
            pl.BlockSpec((1, GROUP_WIDTH), lambda b, g: (0, g)),
            pl.BlockSpec((1, LANES), lambda b, g: (0, b_off + g)),
            pl.BlockSpec((1, LANES), lambda b, g: (0, c_off + g)),
            pl.BlockSpec((1, LANES), lambda b, g: (0, g)),
            pl.BlockSpec((1, LANES), lambda b, g: (0, g)),
            pl.BlockSpec((1, GROUP_WIDTH), lambda b, g: (0, g)),
            pl.BlockSpec((1, GROUP_WIDTH), lambda b, g: (0, g)),
        ],
        out_specs=pl.BlockSpec((s_len, GROUP_WIDTH), g_of),
        out_shape=jax.ShapeDtypeStruct((m, SSD_INNER), F32),
        scratch_shapes=[
            pltpu.VMEM((SSD_CHUNK + 2 * HALO, LANES), F32),
            pltpu.VMEM((s_len, GROUP_WIDTH), F32),
            pltpu.VMEM((s_len, LANES), F32),
            pltpu.VMEM((s_len, LANES), F32),
            pltpu.VMEM((n_chunks,) + state, F32),
            pltpu.VMEM(state, F32),
            pltpu.VMEM(state, F32),
            pltpu.VMEM((n_chunks, SSD_CHUNK, LANES), F32),
            pltpu.VMEM((n_chunks, LANES, SSD_CHUNK), F32),
            pltpu.VMEM((n_chunks, SSD_STATE, SSD_CHUNK), F32),
        ],
        compiler_params=pltpu.CompilerParams(
            dimension_semantics=("parallel", "parallel"), vmem_limit_bytes=VMEM_LIMIT_BYTES),
        name="ssd",
    )(xbc, xbc, xbc, xbc, xbc, xbc, z, dt, conv_w, conv_w, conv_w, conv_b, conv_b, conv_b,
      dt_bias, a_log, d_lanes, norm_g)


def _outproj_body(s_len, a_ref, s_ref, sc_ref, scp_ref, scn_ref, x_ref, w_ref, cw_ref, cb_ref, o_ref, cx):
    tm = x_ref.shape[0]
    blocks_per_seq = s_len // tm
    i = pl.program_id(0) % blocks_per_seq
    w = SC_WIDTH
    gate_cx = lambda ref, rows: ref[rows, w:2 * w] * ref[rows, 2 * w:3 * w]
    cx[HALO:HALO + tm, :] = gate_cx(sc_ref, slice(None))
    cx[HALO - 1:HALO, :] = jnp.where(i == 0, 0.0, gate_cx(scp_ref, slice(HALO - 1, HALO)))
    cx[HALO + tm:HALO + tm + 1, :] = jnp.where(i == blocks_per_seq - 1, 0.0, gate_cx(scn_ref, slice(0, 1)))
    conv = cb_ref[...]
    for k in range(3):
        conv = conv + cw_ref[k:k + 1, :] * cx[HALO - 1 + k:HALO - 1 + k + tm, :]
    c_out = (sc_ref[:, 0:w] * conv).astype(BF16)
    acc = _dot(a_ref[...].astype(BF16), w_ref[0:ATTN_WIDTH, :])
    acc = acc + _dot(s_ref[...].astype(BF16), w_ref[ATTN_WIDTH:ATTN_WIDTH + SSD_INNER, :])
    acc = acc + _dot(c_out, w_ref[ATTN_WIDTH + SSD_INNER:, :])
    o_ref[...] = x_ref[...] + acc


def _out_proj(a, s, sc, x2, w, cw, cb, s_len):
    m = x2.shape[0]
    tm = OUT_TM
    halo_blocks = tm // HALO
    last_halo = m // HALO - 1
    rows = lambda n: pl.BlockSpec((tm, n), lambda i: (i, 0))
    return pl.pallas_call(
        functools.partial(_outproj_body, s_len),
        grid=(m // tm,),
        in_specs=[
            rows(ATTN_WIDTH), rows(SSD_INNER), rows(SC_W),
            pl.BlockSpec((HALO, SC_W), lambda i: (jnp.maximum(i * halo_blocks - 1, 0), 0)),
            pl.BlockSpec((HALO, SC_W), lambda i: (jnp.minimum((i + 1) * halo_blocks, last_halo), 0)),
            rows(D_MODEL),
            pl.BlockSpec(w.shape, lambda i: (0, 0), pipeline_mode=pl.Buffered(1)),
            pl.BlockSpec((3, SC_WIDTH), lambda i: (0, 0)),
            pl.BlockSpec((1, SC_WIDTH), lambda i: (0, 0)),
        ],
        out_specs=rows(D_MODEL),
        out_shape=jax.ShapeDtypeStruct((m, D_MODEL), F32),
        scratch_shapes=[pltpu.VMEM((tm + 2 * HALO, SC_WIDTH), F32)],
        compiler_params=pltpu.CompilerParams(
            dimension_semantics=("parallel",), vmem_limit_bytes=VMEM_LIMIT_BYTES),
        name="out_proj",
    )(a, s, sc, sc, sc, x2, w, cw, cb)


def _ffn_body(s_len, x_ref, xp_ref, xn_ref, g_ref, wu_ref, cw_ref, cb_ref, wd_ref, o_ref, h, ug, uv):
    tm = x_ref.shape[0]
    blocks_per_seq = s_len // tm
    i = pl.program_id(0) % blocks_per_seq

    def norm(x):
        ms = jnp.mean(x * x, axis=-1, keepdims=True)
        return x * lax.rsqrt(ms + EPS) * g_ref[...]

    h[0:tm, :] = norm(x_ref[...]).astype(BF16)
    halo = jnp.concatenate([jnp.where(i == 0, 0.0, norm(xp_ref[...])),
                            jnp.where(i == blocks_per_seq - 1, 0.0, norm(xn_ref[...]))], axis=0)
    h[tm:tm + 2 * HALO, :] = halo.astype(BF16)

    def up(col, width):
        return _dot(h[...], wu_ref[:, col:col + width])

    def conv3(u_scr, u, col, width):
        u_scr[HALO:HALO + tm, 0:width] = u[0:tm]
        u_scr[0:HALO, 0:width] = u[tm:tm + HALO]
        u_scr[HALO + tm:2 * HALO + tm, 0:width] = u[tm + HALO:tm + 2 * HALO]
        out = cb_ref[:, col:col + width]
        for k in range(3):
            out = out + cw_ref[k:k + 1, col:col + width] * u_scr[HALO - 1 + k:HALO - 1 + k + tm, 0:width]
        return out

    cols = [sum(FFN_TILES[:j]) for j in range(len(FFN_TILES))]
    pending = (up(cols[0], FFN_TILES[0]), up(D_FF + cols[0], FFN_TILES[0]))
    for j, width in enumerate(FFN_TILES):
        u_gate, u_val = pending
        if j + 1 < len(FFN_TILES):
            pending = (up(cols[j + 1], FFN_TILES[j + 1]), up(D_FF + cols[j + 1], FFN_TILES[j + 1]))
        gate = conv3(ug.at[j % 2], u_gate, cols[j], width)
        val = conv3(uv.at[j % 2], u_val, D_FF + cols[j], width)
        act = (_silu(gate) * val).astype(BF16)
        down = _dot(act, wd_ref[cols[j]:cols[j] + width, :])
        if j == 0:
            o_ref[...] = x_ref[...] + down
        else:
            o_ref[...] += down


def _ffn(x2, g, wu, cw, cb, wd, s_len):
    m = x2.shape[0]
    tm = FFN_TM
    halo_blocks = tm // HALO
    last_halo = m // HALO - 1
    const = lambda a: pl.BlockSpec(a.shape, lambda i: (0, 0), pipeline_mode=pl.Buffered(1))
    return pl.pallas_call(
        functools.partial(_ffn_body, s_len),
        grid=(m // tm,),
        in_specs=[
            pl.BlockSpec((tm, D_MODEL), lambda i: (i, 0)),
            pl.BlockSpec((HALO, D_MODEL), lambda i: (jnp.maximum(i * halo_blocks - 1, 0), 0)),
            pl.BlockSpec((HALO, D_MODEL), lambda i: (jnp.minimum((i + 1) * halo_blocks, last_halo), 0)),
            pl.BlockSpec((1, D_MODEL), lambda i: (0, 0)),
            const(wu), const(cw), const(cb), const(wd),
        ],
        out_specs=pl.BlockSpec((tm, D_MODEL), lambda i: (i, 0)),
        out_shape=jax.ShapeDtypeStruct((m, D_MODEL), F32),
        scratch_shapes=[
            pltpu.VMEM((tm + 2 * HALO, D_MODEL), BF16),
            pltpu.VMEM((2, tm + 2 * HALO, max(FFN_TILES)), F32),
            pltpu.VMEM((2, tm + 2 * HALO, max(FFN_TILES)), F32),
        ],
        compiler_params=pltpu.CompilerParams(
            dimension_semantics=("parallel",), vmem_limit_bytes=VMEM_LIMIT_BYTES),
        name="conv_ffn",
    )(x2, x2, x2, g, wu, cw, cb, wd)


def _regroup_dt(v):
    lead = v.shape[:-1]
    v = v.reshape(lead + (2, SSD_GROUPS, GROUP_HEADS))
    v = jnp.moveaxis(v, -3, -2).reshape(lead + (SSD_GROUPS, DT_SLOTS))
    v = jnp.tile(v, DT_COPIES)
    v = jnp.pad(v, [(0, 0)] * len(lead) + [(0, 0), (0, LANES - DT_COPIES * DT_SLOTS)])
    return v.reshape(lead + (DT_W,))


def _prep_w_in(w):
    head = 3 * ATTN_WIDTH + SSD_INNER + SSD_XBC
    dt_cols = _regroup_dt(w[:, head:head + 2 * SSD_HEADS])
    return jnp.concatenate([w[:, :head], dt_cols, w[:, head + 2 * SSD_HEADS:]], axis=1).astype(BF16)


def kernel(x, rel_table, norm1_g, w_in, q_norm_g, k_norm_g, ssd_conv_w, ssd_conv_b, ssd_dt_bias, ssd_a_log, ssd_d,
           ssd_norm_g, sc_conv_w, sc_conv_b, w_out, norm2_g, ffn_w_up, ffn_conv_w, ffn_conv_b, ffn_w_down):
    batch, s_len, d_model = x.shape
    assert d_model == D_MODEL and s_len == DILATED_PATTERNS[-1][0]
    assert s_len % OUT_TM == 0 and s_len % FFN_TM == 0 and s_len % IN_TM == 0
    depth = w_in.shape[0]
    x2 = x.reshape(batch * s_len, d_model)
    bias_wide, bias_narrow = _bias_tables(rel_table)
    row = lambda v: v.reshape(1, -1).astype(F32)
    for i in range(depth):
        qkv, z, xbc, dt, sc = _in_proj(x2, row(norm1_g[i]), _prep_w_in(w_in[i]))
        gq = row(jnp.tile(q_norm_g[i], 2)) * (HEAD_DIM ** -0.5)
        gk = row(jnp.tile(k_norm_g[i], 2))
        a_out = _attention(qkv, gq, gk, bias_wide, bias_narrow, batch, s_len)
        s_out = _ssd(xbc, z, dt, ssd_conv_w[i], row(ssd_conv_b[i]),
                     row(_regroup_dt(ssd_dt_bias[i].reshape(-1))), row(_regroup_dt(ssd_a_log[i].reshape(-1))),
                     row(jnp.repeat(ssd_d[i], SSD_HEADDIM)), row(ssd_norm_g[i]), batch, s_len)
        x2 = _out_proj(a_out, s_out, sc, x2, w_out[i].astype(BF16), sc_conv_w[i], row(sc_conv_b[i]), s_len)
        x2 = _ffn(x2, row(norm2_g[i]), ffn_w_up[i].astype(BF16), ffn_conv_w[i], row(ffn_conv_b[i]),
                  ffn_w_down[i].astype(BF16), s_len)
    return x2.reshape(batch, s_len, d_model)
```

```python
import functools
import math

import jax
import jax.numpy as jnp
from jax import lax
from jax.experimental import pallas as pl
from jax.experimental.pallas import tpu as pltpu

F32 = jnp.float32
BF16 = jnp.bfloat16

LANES = 128
VMEM_LIMIT_BYTES = 56 * 1024 * 1024

D_MODEL = 1024
ATTN_HEADS = 8
HEAD_DIM = 64
ATTN_WIDTH = ATTN_HEADS * HEAD_DIM
DILATED_PATTERNS = ((128, 1), (512, 4), (2048, 16))
BAND = 64
REL_BUCKETS = 32
REL_MAX_DISTANCE = 1024
SSD_HEADS = 16
SSD_HEADDIM = 64
SSD_INNER = SSD_HEADS * SSD_HEADDIM
SSD_GROUPS = 2
SSD_STATE = 128
SSD_XBC = SSD_INNER + 2 * SSD_GROUPS * SSD_STATE
SSD_CONV = 7
SSD_CHUNK = 128
GROUP_HEADS = SSD_HEADS // SSD_GROUPS
GROUP_WIDTH = SSD_INNER // SSD_GROUPS
GROUP_PAIRS = GROUP_WIDTH // LANES
SC_WIDTH = 512
D_FF = 2816
EPS = 1e-6
NEG_INF = -1e30
LOG2_E = math.log2(math.e)

QKV_W = 3 * ATTN_WIDTH
DT_W = SSD_GROUPS * LANES
SC_W = 3 * SC_WIDTH
IN_PIECES = (QKV_W, SSD_INNER, SSD_XBC, DT_W, SC_W)

IN_TM = 256
OUT_TM = 512
FFN_TM = 512
FFN_TILES = (1536, 1280)
assert sum(FFN_TILES) == D_FF
HALO = 8
QBLK = 128
KWIN = 2 * QBLK
ATTN_UNROLL = 8
SSD_UNROLL = 2
PREP_UNROLL = 4
DT_SLOTS = 2 * GROUP_HEADS
DT_COPIES = 4
ROW_SRC, ROW_END, ROW_DT, ROW_TOT = (q * DT_SLOTS for q in range(DT_COPIES))


def _split3(x):
    p0 = x.astype(BF16)
    r = x - p0.astype(F32)
    p1 = r.astype(BF16)
    p2 = (r - p1.astype(F32)).astype(BF16)
    return p0, p1, p2


def _dot(a, b):
    return jnp.dot(a, b, preferred_element_type=F32)


def _dot_nt(a, b):
    return lax.dot_general(a, b, (((1,), (1,)), ((), ())), preferred_element_type=F32)


def _silu(x):
    return x / (1.0 + jnp.exp(-x))


def _store_with_halo(u, v, tm):
    for s in range(v.shape[1] // LANES):
        cols = slice(s * LANES, (s + 1) * LANES)
        u[s, HALO:HALO + tm, :] = v[0:tm, cols]
        u[s, 0:HALO, :] = v[tm:tm + HALO, cols]
        u[s, HALO + tm:2 * HALO + tm, :] = v[tm + HALO:tm + 2 * HALO, cols]


def _conv_taps(u_slab, w_ref, b_ref, cols, taps, tm):
    lead = HALO - taps // 2
    acc = b_ref[:, cols]
    for k in range(taps):
        acc = acc + w_ref[k:k + 1, cols] * u_slab[lead + k:lead + k + tm, :]
    return acc


def _inproj_body(s_len, x_ref, xp_ref, xn_ref, g_ref, w_ref, cw_ref, cb_ref,
                 qkv_ref, z_ref, xbc_ref, dt_ref, sc_ref, h, u):
    tm = x_ref.shape[0]
    blocks_per_seq = s_len // tm
    i = pl.program_id(0) % blocks_per_seq

    def norm(x):
        ms = jnp.mean(x * x, axis=-1, keepdims=True)
        return x * lax.rsqrt(ms + EPS) * g_ref[...]

    h[0:tm, :] = norm(x_ref[...]).astype(BF16)
    halo = jnp.concatenate([jnp.where(i == 0, 0.0, norm(xp_ref[...])),
                            jnp.where(i == blocks_per_seq - 1, 0.0, norm(xn_ref[...]))], axis=0)
    h[tm:tm + 2 * HALO, :] = halo.astype(BF16)

    xbc_off = QKV_W + SSD_INNER
    pre = _dot(h[...], w_ref[:, xbc_off:xbc_off + SSD_XBC])
    _store_with_halo(u, pre, tm)
    hb = h[0:tm, :]
    off = 0
    for ref in (qkv_ref, z_ref, None, dt_ref, sc_ref):
        n = SSD_XBC if ref is None else ref.shape[-1]
        if ref is not None:
            ref[...] = _dot(hb, w_ref[:, off:off + n])
        off += n
    for s in range(SSD_XBC // LANES):
        cols = slice(s * LANES, (s + 1) * LANES)
        xbc_ref[:, cols] = _silu(_conv_taps(u.at[s], cw_ref, cb_ref, cols, SSD_CONV, tm))


def _in_proj(x2, g, w, conv_w, conv_b, s_len):
    m = x2.shape[0]
    n_total = sum(IN_PIECES)
    halo_blocks = IN_TM // HALO
    last_halo = m // HALO - 1
    return pl.pallas_call(
        functools.partial(_inproj_body, s_len),
        grid=(m // IN_TM,),
        in_specs=[
            pl.BlockSpec((IN_TM, D_MODEL), lambda i: (i, 0)),
            pl.BlockSpec((HALO, D_MODEL), lambda i: (jnp.maximum(i * halo_blocks - 1, 0), 0)),
            pl.BlockSpec((HALO, D_MODEL), lambda i: (jnp.minimum((i + 1) * halo_blocks, last_halo), 0)),
            pl.BlockSpec((1, D_MODEL), lambda i: (0, 0)),
            pl.BlockSpec((D_MODEL, n_total), lambda i: (0, 0), pipeline_mode=pl.Buffered(1)),
            pl.BlockSpec((SSD_CONV, SSD_XBC), lambda i: (0, 0)),
            pl.BlockSpec((1, SSD_XBC), lambda i: (0, 0)),
        ],
        out_specs=[pl.BlockSpec((IN_TM, n), lambda i: (i, 0)) for n in IN_PIECES],
        out_shape=[jax.ShapeDtypeStruct((m, n), F32) for n in IN_PIECES],
        scratch_shapes=[
            pltpu.VMEM((IN_TM + 2 * HALO, D_MODEL), BF16),
            pltpu.VMEM((SSD_XBC // LANES, IN_TM + 2 * HALO, LANES), F32),
        ],
        compiler_params=pltpu.CompilerParams(
            dimension_semantics=("parallel",), vmem_limit_bytes=VMEM_LIMIT_BYTES),
        name="in_proj",
    )(x2, x2, x2, g, w, conv_w, conv_b)


def _attn_body(q_ref, k_ref, v_ref, gq_ref, gk_ref, bw_ref, bn_ref, o_ref,
               qn, kn, q4, k4, v4, qp0, qp1, kp, vp, res_a, res_b):
    s_len = q_ref.shape[0]
    (_, d_fine), (_, d_mid), (_, d_coarse) = DILATED_PATTERNS
    ratio = d_mid // d_fine
    assert d_fine == 1 and d_coarse == ratio * d_mid
    sub_mid = s_len // d_mid
    sub_coarse = s_len // d_coarse
    assert sub_coarse == QBLK
    lane = lax.broadcasted_iota(jnp.int32, (1, LANES), 1)
    head0 = lane < HEAD_DIM
    rr = lax.broadcasted_iota(jnp.int32, (LANES, LANES), 0)
    cc = lax.broadcasted_iota(jnp.int32, (LANES, LANES), 1)
    same_head = jnp.where((rr < HEAD_DIM) == (cc < HEAD_DIM), 1.0, 0.0).astype(BF16)
    chunk = 256

    for c0 in range(0, s_len, chunk):
        rows = pl.ds(c0, chunk)
        for src, g_ref, dst in ((q_ref, gq_ref, qn), (k_ref, gk_ref, kn)):
            x = src[rows, :]
            sq = x * x
            hi = sq.astype(BF16)
            lo = (sq - hi.astype(F32)).astype(BF16)
            ss = _dot(hi, same_head) + _dot(lo, same_head)
            dst[rows, :] = x * lax.rsqrt(ss * (1.0 / HEAD_DIM) + EPS) * g_ref[...]

    for r in range(ratio):
        for c0 in range(0, sub_mid, chunk):
            src = pl.ds(r + c0 * ratio, chunk, stride=ratio)
            dst = pl.ds(r * sub_mid + c0, chunk)
            q4[dst, :] = qn[src, :]
            k4[dst, :] = kn[src, :]
            v4[dst, :] = v_ref[src, :]

    def write_operands(dst, qv, kv, vv):
        qp0[dst, :] = jnp.where(head0, qv, 0.0).astype(BF16)
        qp1[dst, :] = jnp.where(head0, 0.0, qv).astype(BF16)
        kp[dst, :] = kv.astype(BF16)
        vp[dst, :] = vv.astype(BF16)

    def softmax_units(res, specs):
        scores = []
        for qrows, krows, bias_of_head in specs:
            kw = kp[krows, :]
            for h, qp in enumerate((qp0, qp1)):
                scores.append(_dot_nt(qp[qrows, :], kw) + bias_of_head(h))
        probs = []
        for s in scores:
            m = jnp.max(s, axis=-1, keepdims=True)
            p = jnp.exp(s - m)
            probs.append((p.astype(BF16), m, jnp.sum(p, axis=-1, keepdims=True)))
        for i, (qrows, krows, _) in enumerate(specs):
            vw = vp[krows, :]
            (p0, m0, l0), (p1, m1, l1) = probs[2 * i], probs[2 * i + 1]
            res[0, qrows, :] = jnp.where(head0, _dot(p0, vw), _dot(p1, vw))
            res[1, qrows, :] = jnp.where(head0, m0, m1)
            res[2, qrows, :] = jnp.where(head0, l0, l1)

    def attend_wide(res, pi, sub):
        nb = sub // QBLK

        def units(i, carry):
            specs = []
            for j in range(ATTN_UNROLL):
                u = i * ATTN_UNROLL + j
                t = u % nb
                qrows = pl.ds(pl.multiple_of(u * QBLK, QBLK), QBLK)
                kstart = (u - t) * QBLK + jnp.clip(t * QBLK - BAND, 0, sub - KWIN)
                krows = pl.ds(pl.multiple_of(kstart, BAND), KWIN)
                var = jnp.where(t == 0, 0, jnp.where(t == nb - 1, 2, 1))
                specs.append((qrows, krows, lambda h, var=var: bw_ref[pi, var, h]))
            softmax_units(res, specs)
            return carry
        lax.fori_loop(0, s_len // QBLK // ATTN_UNROLL, units, 0)

    def attend_narrow(res):
        def units(i, carry):
            specs = []
            for j in range(ATTN_UNROLL):
                rows = pl.ds(pl.multiple_of((i * ATTN_UNROLL + j) * QBLK, QBLK), QBLK)
                specs.append((rows, rows, lambda h: bn_ref[h]))
            softmax_units(res, specs)
            return carry
        lax.fori_loop(0, s_len // QBLK // ATTN_UNROLL, units, 0)

    def merged(a0, m0, l0, a1, m1, l1):
        mn = jnp.maximum(m0, m1)
        e0 = jnp.exp(m0 - mn)
        e1 = jnp.exp(m1 - mn)
        return a0 * e0 + a1 * e1, mn, l0 * e0 + l1 * e1

    for r in range(ratio):
        for r2 in range(ratio):
            src = pl.ds(r * sub_mid + r2, sub_coarse, stride=ratio)
            dst = pl.ds((r + ratio * r2) * sub_coarse, sub_coarse)
            write_operands(dst, q4[src, :], k4[src, :], v4[src, :])
    attend_narrow(res_a)

    for c0 in range(0, s_len, chunk):
        rows = pl.ds(c0, chunk)
        write_operands(rows, q4[rows, :], k4[rows, :], v4[rows, :])
    attend_wide(res_b, 1, sub_mid)
    for r in range(ratio):
        for r2 in range(ratio):
            fine = pl.ds(r * sub_mid + r2, sub_coarse, stride=ratio)
            coarse = pl.ds((r + ratio * r2) * sub_coarse, sub_coarse)
            a, m, l = merged(res_b[0, fine, :], res_b[1, fine, :], res_b[2, fine, :],
                             res_a[0, coarse, :], res_a[1, coarse, :], res_a[2, coarse, :])
            res_b[0, fine, :] = a
            res_b[1, fine, :] = m
            res_b[2, fine, :] = l

    for c0 in range(0, s_len, chunk):
        rows = pl.ds(c0, chunk)
        write_operands(rows, qn[rows, :], kn[rows, :], v_ref[rows, :])
    attend_wide(res_a, 0, s_len)
    for r in range(ratio):
        for c0 in range(0, sub_mid, chunk):
            nat = pl.ds(r + c0 * ratio, chunk, stride=ratio)
            mid = pl.ds(r * sub_mid + c0, chunk)
            a, _, l = merged(res_a[0, nat, :], res_a[1, nat, :], res_a[2, nat, :],
                             res_b[0, mid, :], res_b[1, mid, :], res_b[2, mid, :])
            o_ref[nat, :] = a / l


def _attention(qkv, gq, gk, bias_wide, bias_narrow, batch, s_len):
    m = qkv.shape[0]
    pairs = ATTN_WIDTH // LANES
    slab = lambda off: pl.BlockSpec((s_len, LANES), lambda b, p: (b, off + p))
    row = pl.BlockSpec((1, LANES), lambda b, p: (0, 0))
    f32_slab = pltpu.VMEM((s_len, LANES), F32)
    bf16_slab = pltpu.VMEM((s_len, LANES), BF16)
    stats = pltpu.VMEM((3, s_len, LANES), F32)
    return pl.pallas_call(
        _attn_body,
        grid=(batch, pairs),
        in_specs=[
            slab(0), slab(pairs), slab(2 * pairs), row, row,
            pl.BlockSpec((2, 3, 2, QBLK, KWIN), lambda b, p: (0, 0, p, 0, 0)),
            pl.BlockSpec((2, QBLK, QBLK), lambda b, p: (p, 0, 0)),
        ],
        out_specs=pl.BlockSpec((s_len, LANES), lambda b, p: (b, p)),
        out_shape=jax.ShapeDtypeStruct((m, ATTN_WIDTH), F32),
        scratch_shapes=[f32_slab, f32_slab, f32_slab, f32_slab, f32_slab,
                        bf16_slab, bf16_slab, bf16_slab, bf16_slab, stats, stats],
        compiler_params=pltpu.CompilerParams(
            dimension_semantics=("parallel", "parallel"), vmem_limit_bytes=VMEM_LIMIT_BYTES),
        name="dilated_attention",
    )(qkv, qkv, qkv, gq, gk, bias_wide, bias_narrow)


def _t5_bucket(rel):
    nb = REL_BUCKETS // 2
    max_exact = nb // 2
    ret = jnp.where(rel > 0, nb, 0)
    n = jnp.abs(rel)
    nf = jnp.maximum(n, 1).astype(F32)
    large = max_exact + (jnp.log(nf / max_exact) / math.log(REL_MAX_DISTANCE / max_exact)
                         * (nb - max_exact)).astype(jnp.int32)
    large = jnp.minimum(large, nb - 1)
    return ret + jnp.where(n < max_exact, n, large)


def _bias_tables(rel_table):
    period = 2 * KWIN

    def table(d, n_keys, offset):
        k = jnp.arange(period)
        delta = jnp.where(k < KWIN, k, k - period) - offset
        g = rel_table[_t5_bucket(delta * d)].astype(F32)
        g = jnp.where((jnp.abs(delta) <= BAND)[:, None], g, NEG_INF).T
        flat = jnp.tile(g, (1, QBLK))[:, :QBLK * (period - 1)]
        return flat.reshape(ATTN_HEADS, QBLK, period - 1)[:, :, :n_keys]
    wide = jnp.stack([jnp.stack([table(d, KWIN, off) for off in (0, BAND, 2 * BAND)])
                      for (_, d) in DILATED_PATTERNS[:2]])
    narrow = table(DILATED_PATTERNS[2][1], QBLK, 0)
    return wide, narrow


def _ssd_body(xs_c, b_c, c_c, z_ref, dt_ref, dtb_ref, alog_ref, dsk_ref, ng_ref, o_ref,
              sb_in, s_f, s_b, col, rows_t, bt_s):
    s_len = z_ref.shape[0]
    t_len = SSD_CHUNK
    n_chunks = s_len // t_len
    lane = lax.broadcasted_iota(jnp.int32, (1, LANES), 1)
    head0 = lane < SSD_HEADDIM
    fwd_slot = lane % DT_SLOTS < GROUP_HEADS
    copy = lane // DT_SLOTS
    rr = lax.broadcasted_iota(jnp.int32, (t_len, t_len), 0)
    cc = lax.broadcasted_iota(jnp.int32, (t_len, t_len), 1)
    tri_le = jnp.where(cc <= rr, 1.0, 0.0).astype(BF16)
    causal = cc <= rr
    diag = cc == rr
    a_row = -jnp.exp(alog_ref[...]) * LOG2_E

    chunk_rows = lambda c: pl.ds(pl.multiple_of(c * t_len, t_len), t_len)
    pair_cols = lambda pp: slice(pp * LANES, (pp + 1) * LANES)

    def prepare(i, carry):
        for k in range(PREP_UNROLL):
            c = i * PREP_UNROLL + k
            rows = chunk_rows(c)
            v = dt_ref[rows, :] + dtb_ref[...]
            dt_c = jnp.maximum(v, 0.0) + jnp.log1p(jnp.exp(-jnp.abs(v)))
            dta = dt_c * a_row
            p0, p1, p2 = _split3(dta)
            fwd_incl = _dot(tri_le, p0) + _dot(tri_le, p1) + _dot(tri_le, p2)
            total = fwd_incl[t_len - 1:t_len, :]
            incl = jnp.where(fwd_slot, fwd_incl, total - fwd_incl + dta)
            to_edge = jnp.where(fwd_slot, total - fwd_incl, fwd_incl - dta)
            col[c] = incl
            packed = jnp.where(copy == 0, incl - jnp.log2(dt_c),
                               jnp.where(copy == 1, to_edge, jnp.where(copy == 2, dt_c, total)))
            rows_t[c] = packed.T
            bt_s[c] = b_c[rows, :].T
        return carry
    lax.fori_loop(0, n_chunks // PREP_UNROLL, prepare, 0)

    def state_lhs(c, j):
        w = jnp.exp2(rows_t[c, ROW_END + j:ROW_END + j + 1, :]) * rows_t[c, ROW_DT + j:ROW_DT + j + 1, :]
        return (bt_s[c] * w).astype(BF16)

    def chunk_decay(c, pp, base):
        j = ROW_TOT + base + 2 * pp
        return jnp.where(head0, jnp.exp2(rows_t[c, j:j + 1, :]), jnp.exp2(rows_t[c, j + 1:j + 2, :]))

    s_b[...] = jnp.zeros_like(s_b)
    s_f[...] = jnp.zeros_like(s_f)

    def bwd_states(i, carry):
        chunks = [n_chunks - 1 - (i * SSD_UNROLL + k) for k in range(SSD_UNROLL)]
        local = []
        for c in chunks:
            rows = chunk_rows(c)
            lhs = [state_lhs(c, GROUP_HEADS + e) for e in range(GROUP_HEADS)]
            st = []
            for pp in range(GROUP_PAIRS):
                xsp = xs_c[rows, pair_cols(pp)].astype(BF16)
                st.append(jnp.where(head0, _dot(lhs[2 * pp], xsp), _dot(lhs[2 * pp + 1], xsp)))
            local.append(st)
        for k, c in enumerate(chunks):
            for pp in range(GROUP_PAIRS):
                sb_in[c, pp] = s_b[pp]
                s_b[pp] = s_b[pp] * chunk_decay(c, pp, GROUP_HEADS) + local[k][pp]
        return carry
    lax.fori_loop(0, n_chunks // SSD_UNROLL, bwd_states, 0)

    def outputs(i, carry):
        chunks = [i * SSD_UNROLL + k for k in range(SSD_UNROLL)]
        staged = []
        for c in chunks:
            rows = chunk_rows(c)
            c_b = c_c[rows, :].astype(BF16)
            cb = _dot(c_b, bt_s[c].astype(BF16))
            incl = col[c]
            lane_bcast = lambda j: jnp.broadcast_to(incl[:, j:j + 1], (t_len, t_len))
            src_row = lambda j: rows_t[c, ROW_SRC + j:ROW_SRC + j + 1, :]
            mix, col_f, col_b = [], [], []
            for e in range(GROUP_HEADS):
                jf, jb = e, GROUP_HEADS + e
                cf, cbk = lane_bcast(jf), lane_bcast(jb)
                w = jnp.exp2(jnp.where(causal, cf - src_row(jf), cbk - src_row(jb)))
                w = w + jnp.where(diag, rows_t[c, ROW_DT + jb:ROW_DT + jb + 1, :], 0.0)
                mix.append((cb * w).astype(BF16))
                col_f.append(cf)
                col_b.append(cbk)
            lhs_f = [state_lhs(c, e) for e in range(GROUP_HEADS)]
            per_pair = []
            for pp in range(GROUP_PAIRS):
                xs_f = xs_c[rows, pair_cols(pp)]
                xsp = xs_f.astype(BF16)
                y = jnp.where(head0, _dot(mix[2 * pp], xsp), _dot(mix[2 * pp + 1], xsp))
                y = y + (_dot(c_b, sb_in[c, pp].astype(BF16))
                         * jnp.exp2(jnp.where(head0, col_b[2 * pp], col_b[2 * pp + 1])))
                y = y + dsk_ref[:, pair_cols(pp)] * xs_f
                st = jnp.where(head0, _dot(lhs_f[2 * pp], xsp), _dot(lhs_f[2 * pp + 1], xsp))
                scale_f = jnp.exp2(jnp.where(head0, col_f[2 * pp], col_f[2 * pp + 1]))
                per_pair.append((y, st, scale_f))
            staged.append((c, rows, c_b, per_pair))
        for c, rows, c_b, per_pair in staged:
            sq = jnp.zeros((t_len, LANES), F32)
            for pp, (y, st, scale_f) in enumerate(per_pair):
                y = y + _dot(c_b, s_f[pp].astype(BF16)) * scale_f
                s_f[pp] = s_f[pp] * chunk_decay(c, pp, 0) + st
                yz = y * _silu(z_ref[rows, pair_cols(pp)])
                sq = sq + yz * yz
                o_ref[rows, pair_cols(pp)] = yz
            scale = lax.rsqrt(jnp.sum(sq, axis=-1, keepdims=True) * (1.0 / GROUP_WIDTH) + EPS)
            o_ref[rows, :] = o_ref[rows, :] * scale * ng_ref[...]
        return carry
    lax.fori_loop(0, n_chunks // SSD_UNROLL, outputs, 0)


def _ssd(xbc, z, dt, dt_bias, a_log, d_lanes, norm_g, batch, s_len):
    m = xbc.shape[0]
    g_of = lambda b, g: (b, g)
    b_off = SSD_INNER // LANES
    c_off = b_off + SSD_GROUPS
    n_chunks = s_len // SSD_CHUNK
    state = (GROUP_PAIRS, SSD_STATE, LANES)
    return pl.pallas_call(
        _ssd_body,
        grid=(batch, SSD_GROUPS),
        in_specs=[
            pl.BlockSpec((s_len, GROUP_WIDTH), g_of),
            pl.BlockSpec((s_len, LANES), lambda b, g: (b, b_off + g)),
            pl.BlockSpec((s_len, LANES), lambda b, g: (b, c_off + g)),
            pl.BlockSpec((s_len, GROUP_WIDTH), g_of),
            pl.BlockSpec((s_len, LANES), g_of),
            pl.BlockSpec((1, LANES), lambda b, g: (0, g)),
            pl.BlockSpec((1, LANES), lambda b, g: (0, g)),
            pl.BlockSpec((1, GROUP_WIDTH), lambda b, g: (0, g)),
            pl.BlockSpec((1, GROUP_WIDTH), lambda b, g: (0, g)),
        ],
        out_specs=pl.BlockSpec((s_len, GROUP_WIDTH), g_of),
        out_shape=jax.ShapeDtypeStruct((m, SSD_INNER), F32),
        scratch_shapes=[
            pltpu.VMEM((n_chunks,) + state, F32),
            pltpu.VMEM(state, F32),
            pltpu.VMEM(state, F32),
            pltpu.VMEM((n_chunks, SSD_CHUNK, LANES), F32),
            pltpu.VMEM((n_chunks, LANES, SSD_CHUNK), F32),
            pltpu.VMEM((n_chunks, SSD_STATE, SSD_CHUNK), F32),
        ],
        compiler_params=pltpu.CompilerParams(
            dimension_semantics=("parallel", "parallel"), vmem_limit_bytes=VMEM_LIMIT_BYTES),
        name="ssd",
    )(xbc, xbc, xbc, z, dt, dt_bias, a_log, d_lanes, norm_g)


def _outproj_body(s_len, a_ref, s_ref, sc_ref, scp_ref, scn_ref, x_ref, w_ref, cw_ref, cb_ref, o_ref, cx):
    tm = x_ref.shape[0]
    blocks_per_seq = s_len // tm
    i = pl.program_id(0) % blocks_per_seq
    w = SC_WIDTH
    gate_cx = lambda ref, rows: ref[rows, w:2 * w] * ref[rows, 2 * w:3 * w]
    cx[HALO:HALO + tm, :] = gate_cx(sc_ref, slice(None))
    cx[HALO - 1:HALO, :] = jnp.where(i == 0, 0.0, gate_cx(scp_ref, slice(HALO - 1, HALO)))
    cx[HALO + tm:HALO + tm + 1, :] = jnp.where(i == blocks_per_seq - 1, 0.0, gate_cx(scn_ref, slice(0, 1)))
    conv = cb_ref[...]
    for k in range(3):
        conv = conv + cw_ref[k:k + 1, :] * cx[HALO - 1 + k:HALO - 1 + k + tm, :]
    c_out = (sc_ref[:, 0:w] * conv).astype(BF16)
    acc = _dot(a_ref[...].astype(BF16), w_ref[0:ATTN_WIDTH, :])
    acc = acc + _dot(s_ref[...].astype(BF16), w_ref[ATTN_WIDTH:ATTN_WIDTH + SSD_INNER, :])
    acc = acc + _dot(c_out, w_ref[ATTN_WIDTH + SSD_INNER:, :])
    o_ref[...] = x_ref[...] + acc


def _out_proj(a, s, sc, x2, w, cw, cb, s_len):
    m = x2.shape[0]
    tm = OUT_TM
    halo_blocks = tm // HALO
    last_halo = m // HALO - 1
    rows = lambda n: pl.BlockSpec((tm, n), lambda i: (i, 0))
    return pl.pallas_call(
        functools.partial(_outproj_body, s_len),
        grid=(m // tm,),
        in_specs=[
            rows(ATTN_WIDTH), rows(SSD_INNER), rows(SC_W),
            pl.BlockSpec((HALO, SC_W), lambda i: (jnp.maximum(i * halo_blocks - 1, 0), 0)),
            pl.BlockSpec((HALO, SC_W), lambda i: (jnp.minimum((i + 1) * halo_blocks, last_halo), 0)),
            rows(D_MODEL),
            pl.BlockSpec(w.shape, lambda i: (0, 0), pipeline_mode=pl.Buffered(1)),
            pl.BlockSpec((3, SC_WIDTH), lambda i: (0, 0)),
            pl.BlockSpec((1, SC_WIDTH), lambda i: (0, 0)),
        ],
        out_specs=rows(D_MODEL),
        out_shape=jax.ShapeDtypeStruct((m, D_MODEL), F32),
        scratch_shapes=[pltpu.VMEM((tm + 2 * HALO, SC_WIDTH), F32)],
        compiler_params=pltpu.CompilerParams(
            dimension_semantics=("parallel",), vmem_limit_bytes=VMEM_LIMIT_BYTES),
        name="out_proj",
    )(a, s, sc, sc, sc, x2, w, cw, cb)


def _ffn_body(s_len, x_ref, xp_ref, xn_ref, g_ref, wu_ref, cw_ref, cb_ref, wd_ref, o_ref, h, ug, uv):
    tm = x_ref.shape[0]
    blocks_per_seq = s_len // tm
    i = pl.program_id(0) % blocks_per_seq

    def norm(x):
        ms = jnp.mean(x * x, axis=-1, keepdims=True)
        return x * lax.rsqrt(ms + EPS) * g_ref[...]

    h[0:tm, :] = norm(x_ref[...]).astype(BF16)
    halo = jnp.concatenate([jnp.where(i == 0, 0.0, norm(xp_ref[...])),
                            jnp.where(i == blocks_per_seq - 1, 0.0, norm(xn_ref[...]))], axis=0)
    h[tm:tm + 2 * HALO, :] = halo.astype(BF16)

    def up(col, width):
        return _dot(h[...], wu_ref[:, col:col + width])

    def conv3(u_scr, u, col, width):
        _store_with_halo(u_scr, u, tm)
        return jnp.concatenate(
            [_conv_taps(u_scr.at[s], cw_ref, cb_ref, slice(col + s * LANES, col + (s + 1) * LANES), 3, tm)
             for s in range(width // LANES)], axis=1)

    cols = [sum(FFN_TILES[:j]) for j in range(len(FFN_TILES))]
    pending = (up(cols[0], FFN_TILES[0]), up(D_FF + cols[0], FFN_TILES[0]))
    for j, width in enumerate(FFN_TILES):
        u_gate, u_val = pending
        if j + 1 < len(FFN_TILES):
            pending = (up(cols[j + 1], FFN_TILES[j + 1]), up(D_FF + cols[j + 1], FFN_TILES[j + 1]))
        gate = conv3(ug.at[j % 2], u_gate, cols[j], width)
        val = conv3(uv.at[j % 2], u_val, D_FF + cols[j], width)
        act = (_silu(gate) * val).astype(BF16)
        down = _dot(act, wd_ref[cols[j]:cols[j] + width, :])
        if j == 0:
            o_ref[...] = x_ref[...] + down
        else:
            o_ref[...] += down


def _ffn(x2, g, wu, cw, cb, wd, s_len):
    m = x2.shape[0]
    tm = FFN_TM
    halo_blocks = tm // HALO
    last_halo = m // HALO - 1
    const = lambda a: pl.BlockSpec(a.shape, lambda i: (0, 0), pipeline_mode=pl.Buffered(1))
    return pl.pallas_call(
        functools.partial(_ffn_body, s_len),
        grid=(m // tm,),
        in_specs=[
            pl.BlockSpec((tm, D_MODEL), lambda i: (i, 0)),
            pl.BlockSpec((HALO, D_MODEL), lambda i: (jnp.maximum(i * halo_blocks - 1, 0), 0)),
            pl.BlockSpec((HALO, D_MODEL), lambda i: (jnp.minimum((i + 1) * halo_blocks, last_halo), 0)),
            pl.BlockSpec((1, D_MODEL), lambda i: (0, 0)),
            const(wu), const(cw), const(cb), const(wd),
        ],
        out_specs=pl.BlockSpec((tm, D_MODEL), lambda i: (i, 0)),
        out_shape=jax.ShapeDtypeStruct((m, D_MODEL), F32),
        scratch_shapes=[
            pltpu.VMEM((tm + 2 * HALO, D_MODEL), BF16),
            pltpu.VMEM((2, max(FFN_TILES) // LANES, tm + 2 * HALO, LANES), F32),
            pltpu.VMEM((2, max(FFN_TILES) // LANES, tm + 2 * HALO, LANES), F32),
        ],
        compiler_params=pltpu.CompilerParams(
            dimension_semantics=("parallel",), vmem_limit_bytes=VMEM_LIMIT_BYTES),
        name="conv_ffn",
    )(x2, x2, x2, g, wu, cw, cb, wd)


def _regroup_dt(v):
    lead = v.shape[:-1]
    v = v.reshape(lead + (2, SSD_GROUPS, GROUP_HEADS))
    v = jnp.moveaxis(v, -3, -2).reshape(lead + (SSD_GROUPS, DT_SLOTS))
    v = jnp.tile(v, DT_COPIES)
    v = jnp.pad(v, [(0, 0)] * len(lead) + [(0, 0), (0, LANES - DT_COPIES * DT_SLOTS)])
    return v.reshape(lead + (DT_W,))


def _prep_w_in(w):
    head = 3 * ATTN_WIDTH + SSD_INNER + SSD_XBC
    dt_cols = _regroup_dt(w[:, head:head + 2 * SSD_HEADS])
    return jnp.concatenate([w[:, :head], dt_cols, w[:, head + 2 * SSD_HEADS:]], axis=1).astype(BF16)


def kernel(x, rel_table, norm1_g, w_in, q_norm_g, k_norm_g, ssd_conv_w, ssd_conv_b, ssd_dt_bias, ssd_a_log, ssd_d,
           ssd_norm_g, sc_conv_w, sc_conv_b, w_out, norm2_g, ffn_w_up, ffn_conv_w, ffn_conv_b, ffn_w_down):
    batch, s_len, d_model = x.shape
    assert d_model == D_MODEL and s_len == DILATED_PATTERNS[-1][0]
    assert s_len % OUT_TM == 0 and s_len % FFN_TM == 0 and s_len % IN_TM == 0
    depth = w_in.shape[0]
    x2 = x.reshape(batch * s_len, d_model)
    bias_wide, bias_narrow = _bias_tables(rel_table)
    row = lambda v: v.reshape(1, -1).astype(F32)
    for i in range(depth):
        qkv, z, xbc, dt, sc = _in_proj(x2, row(norm1_g[i]), _prep_w_in(w_in[i]), ssd_conv_w[i],
                                       row(ssd_conv_b[i]), s_len)
        gq = row(jnp.tile(q_norm_g[i], 2)) * (HEAD_DIM ** -0.5)
        gk = row(jnp.tile(k_norm_g[i], 2))
        a_out = _attention(qkv, gq, gk, bias_wide, bias_narrow, batch, s_len)
        s_out = _ssd(xbc, z, dt,
                     row(_regroup_dt(ssd_dt_bias[i].reshape(-1))), row(_regroup_dt(ssd_a_log[i].reshape(-1))),
                     row(jnp.repeat(ssd_d[i], SSD_HEADDIM)), row(ssd_norm_g[i]), batch, s_len)
        x2 = _out_proj(a_out, s_out, sc, x2, w_out[i].astype(BF16), sc_conv_w[i], row(sc_conv_b[i]), s_len)
        x2 = _ffn(x2, row(norm2_g[i]), ffn_w_up[i].astype(BF16), ffn_conv_w[i], row(ffn_conv_b[i]),
                  ffn_w_down[i].astype(BF16), s_len)
    return x2.reshape(batch, s_len, d_model)
```

```python
import functools
import math

import jax
import jax.numpy as jnp
from jax import lax
from jax.experimental import pallas as pl
from jax.experimental.pallas import tpu as pltpu

F32 = jnp.float32
BF16 = jnp.bfloat16

LANES = 128
VMEM_LIMIT_BYTES = 56 * 1024 * 1024

D_MODEL = 1024
ATTN_HEADS = 8
HEAD_DIM = 64
ATTN_WIDTH = ATTN_HEADS * HEAD_DIM
DILATED_PATTERNS = ((128, 1), (512, 4), (2048, 16))
BAND = 64
REL_BUCKETS = 32
REL_MAX_DISTANCE = 1024
SSD_HEADS = 16
SSD_HEADDIM = 64
SSD_INNER = SSD_HEADS * SSD_HEADDIM
SSD_GROUPS = 2
SSD_STATE = 128
SSD_XBC = SSD_INNER + 2 * SSD_GROUPS * SSD_STATE
SSD_CONV = 7
SSD_CHUNK = 128
GROUP_HEADS = SSD_HEADS // SSD_GROUPS
GROUP_WIDTH = SSD_INNER // SSD_GROUPS
GROUP_PAIRS = GROUP_WIDTH // LANES
SC_WIDTH = 512
D_FF = 2816
EPS = 1e-6
NEG_INF = -1e30
LOG2_E = math.log2(math.e)

QKV_W = 3 * ATTN_WIDTH
DT_W = SSD_GROUPS * LANES
SC_W = 3 * SC_WIDTH
IN_PIECES = (QKV_W, SSD_INNER, SSD_XBC, DT_W, SC_W)

IN_TM = 256
OUT_TM = 512
FFN_TM = 512
FFN_TILES = (1536, 1280)
assert sum(FFN_TILES) == D_FF
HALO = 8
QBLK = 128
KWIN = 2 * QBLK
ATTN_UNROLL = 16
SSD_UNROLL = 4
PREP_UNROLL = 8
DT_SLOTS = 2 * GROUP_HEADS
DT_COPIES = 4
ROW_SRC, ROW_END, ROW_DT, ROW_TOT = (q * DT_SLOTS for q in range(DT_COPIES))


def _split3(x):
    p0 = x.astype(BF16)
    r = x - p0.astype(F32)
    p1 = r.astype(BF16)
    p2 = (r - p1.astype(F32)).astype(BF16)
    return p0, p1, p2


def _dot(a, b):
    return jnp.dot(a, b, preferred_element_type=F32)


def _dot_nt(a, b):
    return lax.dot_general(a, b, (((1,), (1,)), ((), ())), preferred_element_type=F32)


def _silu(x):
    return x / (1.0 + jnp.exp(-x))


def _store_with_halo(u, v, tm):
    for s in range(v.shape[1] // LANES):
        cols = slice(s * LANES, (s + 1) * LANES)
        u[s, HALO:HALO + tm, :] = v[0:tm, cols]
        u[s, 0:HALO, :] = v[tm:tm + HALO, cols]
        u[s, HALO + tm:2 * HALO + tm, :] = v[tm + HALO:tm + 2 * HALO, cols]


def _conv_taps(u_slab, w_ref, b_ref, cols, taps, tm):
    lead = HALO - taps // 2
    acc = b_ref[:, cols]
    for k in range(taps):
        acc = acc + w_ref[k:k + 1, cols] * u_slab[lead + k:lead + k + tm, :]
    return acc


def _inproj_body(s_len, x_ref, xp_ref, xn_ref, g_ref, w_ref, cw_ref, cb_ref,
                 qkv_ref, z_ref, xbc_ref, dt_ref, sc_ref, h, u):
    tm = x_ref.shape[0]
    blocks_per_seq = s_len // tm
    i = pl.program_id(0) % blocks_per_seq

    def norm(x):
        ms = jnp.mean(x * x, axis=-1, keepdims=True)
        return x * lax.rsqrt(ms + EPS) * g_ref[...]

    h[0:tm, :] = norm(x_ref[...]).astype(BF16)
    halo = jnp.concatenate([jnp.where(i == 0, 0.0, norm(xp_ref[...])),
                            jnp.where(i == blocks_per_seq - 1, 0.0, norm(xn_ref[...]))], axis=0)
    h[tm:tm + 2 * HALO, :] = halo.astype(BF16)

    xbc_off = QKV_W + SSD_INNER
    pre = _dot(h[...], w_ref[:, xbc_off:xbc_off + SSD_XBC])
    _store_with_halo(u, pre, tm)
    hb = h[0:tm, :]
    off = 0
    for ref in (qkv_ref, z_ref, None, dt_ref, sc_ref):
        n = SSD_XBC if ref is None else ref.shape[-1]
        if ref is not None:
            ref[...] = _dot(hb, w_ref[:, off:off + n])
        off += n
    for s in range(SSD_XBC // LANES):
        cols = slice(s * LANES, (s + 1) * LANES)
        xbc_ref[:, cols] = _silu(_conv_taps(u.at[s], cw_ref, cb_ref, cols, SSD_CONV, tm))


def _in_proj(x2, g, w, conv_w, conv_b, s_len):
    m = x2.shape[0]
    n_total = sum(IN_PIECES)
    halo_blocks = IN_TM // HALO
    last_halo = m // HALO - 1
    return pl.pallas_call(
        functools.partial(_inproj_body, s_len),
        grid=(m // IN_TM,),
        in_specs=[
            pl.BlockSpec((IN_TM, D_MODEL), lambda i: (i, 0)),
            pl.BlockSpec((HALO, D_MODEL), lambda i: (jnp.maximum(i * halo_blocks - 1, 0), 0)),
            pl.BlockSpec((HALO, D_MODEL), lambda i: (jnp.minimum((i + 1) * halo_blocks, last_halo), 0)),
            pl.BlockSpec((1, D_MODEL), lambda i: (0, 0)),
            pl.BlockSpec((D_MODEL, n_total), lambda i: (0, 0), pipeline_mode=pl.Buffered(1)),
            pl.BlockSpec((SSD_CONV, SSD_XBC), lambda i: (0, 0)),
            pl.BlockSpec((1, SSD_XBC), lambda i: (0, 0)),
        ],
        out_specs=[pl.BlockSpec((IN_TM, n), lambda i: (i, 0)) for n in IN_PIECES],
        out_shape=[jax.ShapeDtypeStruct((m, n), F32) for n in IN_PIECES],
        scratch_shapes=[
            pltpu.VMEM((IN_TM + 2 * HALO, D_MODEL), BF16),
            pltpu.VMEM((SSD_XBC // LANES, IN_TM + 2 * HALO, LANES), F32),
        ],
        compiler_params=pltpu.CompilerParams(
            dimension_semantics=("parallel",), vmem_limit_bytes=VMEM_LIMIT_BYTES),
        name="in_proj",
    )(x2, x2, x2, g, w, conv_w, conv_b)


def _attn_body(q_ref, k_ref, v_ref, gq_ref, gk_ref, bw_ref, bn_ref, o_ref,
               qn, kn, q4, k4, v4, qp0, qp1, kp, vp, res_a, res_b):
    s_len = q_ref.shape[0]
    (_, d_fine), (_, d_mid), (_, d_coarse) = DILATED_PATTERNS
    ratio = d_mid // d_fine
    assert d_fine == 1 and d_coarse == ratio * d_mid
    sub_mid = s_len // d_mid
    sub_coarse = s_len // d_coarse
    assert sub_coarse == QBLK
    lane = lax.broadcasted_iota(jnp.int32, (1, LANES), 1)
    head0 = lane < HEAD_DIM
    rr = lax.broadcasted_iota(jnp.int32, (LANES, LANES), 0)
    cc = lax.broadcasted_iota(jnp.int32, (LANES, LANES), 1)
    same_head = jnp.where((rr < HEAD_DIM) == (cc < HEAD_DIM), 1.0, 0.0).astype(BF16)
    chunk = 256

    for c0 in range(0, s_len, chunk):
        rows = pl.ds(c0, chunk)
        for src, g_ref, dst in ((q_ref, gq_ref, qn), (k_ref, gk_ref, kn)):
            x = src[rows, :]
            sq = x * x
            hi = sq.astype(BF16)
            lo = (sq - hi.astype(F32)).astype(BF16)
            ss = _dot(hi, same_head) + _dot(lo, same_head)
            dst[rows, :] = x * lax.rsqrt(ss * (1.0 / HEAD_DIM) + EPS) * g_ref[...]

    for r in range(ratio):
        for c0 in range(0, sub_mid, chunk):
            src = pl.ds(r + c0 * ratio, chunk, stride=ratio)
            dst = pl.ds(r * sub_mid + c0, chunk)
            q4[dst, :] = qn[src, :]
            k4[dst, :] = kn[src, :]
            v4[dst, :] = v_ref[src, :]

    def write_operands(dst, qv, kv, vv):
        qp0[dst, :] = jnp.where(head0, qv, 0.0).astype(BF16)
        qp1[dst, :] = jnp.where(head0, 0.0, qv).astype(BF16)
        kp[dst, :] = kv.astype(BF16)
        vp[dst, :] = vv.astype(BF16)

    def softmax_units(res, specs):
        scores = []
        for qrows, krows, bias_of_head in specs:
            kw = kp[krows, :]
            for h, qp in enumerate((qp0, qp1)):
                scores.append(_dot_nt(qp[qrows, :], kw) + bias_of_head(h))
        probs = []
        for s in scores:
            m = jnp.max(s, axis=-1, keepdims=True)
            p = jnp.exp(s - m)
            probs.append((p.astype(BF16), m, jnp.sum(p, axis=-1, keepdims=True)))
        for i, (qrows, krows, _) in enumerate(specs):
            vw = vp[krows, :]
            (p0, m0, l0), (p1, m1, l1) = probs[2 * i], probs[2 * i + 1]
            res[0, qrows, :] = jnp.where(head0, _dot(p0, vw), _dot(p1, vw))
            res[1, qrows, :] = jnp.where(head0, m0, m1)
            res[2, qrows, :] = jnp.where(head0, l0, l1)

    def attend_wide(res, pi, sub):
        nb = sub // QBLK

        def units(i, carry):
            specs = []
            for j in range(ATTN_UNROLL):
                u = i * ATTN_UNROLL + j
                t = u % nb
                qrows = pl.ds(pl.multiple_of(u * QBLK, QBLK), QBLK)
                kstart = (u - t) * QBLK + jnp.clip(t * QBLK - BAND, 0, sub - KWIN)
                krows = pl.ds(pl.multiple_of(kstart, BAND), KWIN)
                var = jnp.where(t == 0, 0, jnp.where(t == nb - 1, 2, 1))
                specs.append((qrows, krows, lambda h, var=var: bw_ref[pi, var, h]))
            softmax_units(res, specs)
            return carry
        lax.fori_loop(0, s_len // QBLK // ATTN_UNROLL, units, 0)

    def attend_narrow(res):
        def units(i, carry):
            specs = []
            for j in range(ATTN_UNROLL):
                rows = pl.ds(pl.multiple_of((i * ATTN_UNROLL + j) * QBLK, QBLK), QBLK)
                specs.append((rows, rows, lambda h: bn_ref[h]))
            softmax_units(res, specs)
            return carry
        lax.fori_loop(0, s_len // QBLK // ATTN_UNROLL, units, 0)

    def merged(a0, m0, l0, a1, m1, l1):
        mn = jnp.maximum(m0, m1)
        e0 = jnp.exp(m0 - mn)
        e1 = jnp.exp(m1 - mn)
        return a0 * e0 + a1 * e1, mn, l0 * e0 + l1 * e1

    for r in range(ratio):
        for r2 in range(ratio):
            src = pl.ds(r * sub_mid + r2, sub_coarse, stride=ratio)
            dst = pl.ds((r + ratio * r2) * sub_coarse, sub_coarse)
            write_operands(dst, q4[src, :], k4[src, :], v4[src, :])
    attend_narrow(res_a)

    for c0 in range(0, s_len, chunk):
        rows = pl.ds(c0, chunk)
        write_operands(rows, q4[rows, :], k4[rows, :], v4[rows, :])
    attend_wide(res_b, 1, sub_mid)
    for r in range(ratio):
        for r2 in range(ratio):
            fine = pl.ds(r * sub_mid + r2, sub_coarse, stride=ratio)
            coarse = pl.ds((r + ratio * r2) * sub_coarse, sub_coarse)
            a, m, l = merged(res_b[0, fine, :], res_b[1, fine, :], res_b[2, fine, :],
                             res_a[0, coarse, :], res_a[1, coarse, :], res_a[2, coarse, :])
            res_b[0, fine, :] = a
            res_b[1, fine, :] = m
            res_b[2, fine, :] = l

    for c0 in range(0, s_len, chunk):
        rows = pl.ds(c0, chunk)
        write_operands(rows, qn[rows, :], kn[rows, :], v_ref[rows, :])
    attend_wide(res_a, 0, s_len)
    for r in range(ratio):
        for c0 in range(0, sub_mid, chunk):
            nat = pl.ds(r + c0 * ratio, chunk, stride=ratio)
            mid = pl.ds(r * sub_mid + c0, chunk)
            a, _, l = merged(res_a[0, nat, :], res_a[1, nat, :], res_a[2, nat, :],
                             res_b[0, mid, :], res_b[1, mid, :], res_b[2, mid, :])
            o_ref[nat, :] = a / l


def _attention(qkv, gq, gk, bias_wide, bias_narrow, batch, s_len):
    m = qkv.shape[0]
    pairs = ATTN_WIDTH // LANES
    slab = lambda off: pl.BlockSpec((s_len, LANES), lambda b, p: (b, off + p))
    row = pl.BlockSpec((1, LANES), lambda b, p: (0, 0))
    f32_slab = pltpu.VMEM((s_len, LANES), F32)
    bf16_slab = pltpu.VMEM((s_len, LANES), BF16)
    stats = pltpu.VMEM((3, s_len, LANES), F32)
    return pl.pallas_call(
        _attn_body,
        grid=(batch, pairs),
        in_specs=[
            slab(0), slab(pairs), slab(2 * pairs), row, row,
            pl.BlockSpec((2, 3, 2, QBLK, KWIN), lambda b, p: (0, 0, p, 0, 0)),
            pl.BlockSpec((2, QBLK, QBLK), lambda b, p: (p, 0, 0)),
        ],
        out_specs=pl.BlockSpec((s_len, LANES), lambda b, p: (b, p)),
        out_shape=jax.ShapeDtypeStruct((m, ATTN_WIDTH), F32),
        scratch_shapes=[f32_slab, f32_slab, f32_slab, f32_slab, f32_slab,
                        bf16_slab, bf16_slab, bf16_slab, bf16_slab, stats, stats],
        compiler_params=pltpu.CompilerParams(
            dimension_semantics=("parallel", "parallel"), vmem_limit_bytes=VMEM_LIMIT_BYTES),
        name="dilated_attention",
    )(qkv, qkv, qkv, gq, gk, bias_wide, bias_narrow)


def _t5_bucket(rel):
    nb = REL_BUCKETS // 2
    max_exact = nb // 2
    ret = jnp.where(rel > 0, nb, 0)
    n = jnp.abs(rel)
    nf = jnp.maximum(n, 1).astype(F32)
    large = max_exact + (jnp.log(nf / max_exact) / math.log(REL_MAX_DISTANCE / max_exact)
                         * (nb - max_exact)).astype(jnp.int32)
    large = jnp.minimum(large, nb - 1)
    return ret + jnp.where(n < max_exact, n, large)


def _bias_tables(rel_table):
    period = 2 * KWIN

    def table(d, n_keys, offset):
        k = jnp.arange(period)
        delta = jnp.where(k < KWIN, k, k - period) - offset
        g = rel_table[_t5_bucket(delta * d)].astype(F32)
        g = jnp.where((jnp.abs(delta) <= BAND)[:, None], g, NEG_INF).T
        flat = jnp.tile(g, (1, QBLK))[:, :QBLK * (period - 1)]
        return flat.reshape(ATTN_HEADS, QBLK, period - 1)[:, :, :n_keys]
    wide = jnp.stack([jnp.stack([table(d, KWIN, off) for off in (0, BAND, 2 * BAND)])
                      for (_, d) in DILATED_PATTERNS[:2]])
    narrow = table(DILATED_PATTERNS[2][1], QBLK, 0)
    return wide, narrow


def _ssd_body(xs_c, b_c, c_c, z_ref, dt_ref, dtb_ref, alog_ref, dsk_ref, ng_ref, o_ref,
              sb_in, s_f, s_b, col, rows_t, bt_s):
    s_len = z_ref.shape[0]
    t_len = SSD_CHUNK
    n_chunks = s_len // t_len
    lane = lax.broadcasted_iota(jnp.int32, (1, LANES), 1)
    head0 = lane < SSD_HEADDIM
    fwd_slot = lane % DT_SLOTS < GROUP_HEADS
    copy = lane // DT_SLOTS
    rr = lax.broadcasted_iota(jnp.int32, (t_len, t_len), 0)
    cc = lax.broadcasted_iota(jnp.int32, (t_len, t_len), 1)
    tri_le = jnp.where(cc <= rr, 1.0, 0.0).astype(BF16)
    causal = cc <= rr
    diag = cc == rr
    a_row = -jnp.exp(alog_ref[...]) * LOG2_E

    chunk_rows = lambda c: pl.ds(pl.multiple_of(c * t_len, t_len), t_len)
    pair_cols = lambda pp: slice(pp * LANES, (pp + 1) * LANES)

    def prepare(i, carry):
        for k in range(PREP_UNROLL):
            c = i * PREP_UNROLL + k
            rows = chunk_rows(c)
            v = dt_ref[rows, :] + dtb_ref[...]
            dt_c = jnp.maximum(v, 0.0) + jnp.log1p(jnp.exp(-jnp.abs(v)))
            dta = dt_c * a_row
            p0, p1, p2 = _split3(dta)
            fwd_incl = _dot(tri_le, p0) + _dot(tri_le, p1) + _dot(tri_le, p2)
            total = fwd_incl[t_len - 1:t_len, :]
            incl = jnp.where(fwd_slot, fwd_incl, total - fwd_incl + dta)
            to_edge = jnp.where(fwd_slot, total - fwd_incl, fwd_incl - dta)
            col[c] = incl
            packed = jnp.where(copy == 0, incl - jnp.log2(dt_c),
                               jnp.where(copy == 1, to_edge, jnp.where(copy == 2, dt_c, total)))
            rows_t[c] = packed.T
            bt_s[c] = b_c[rows, :].T
        return carry
    lax.fori_loop(0, n_chunks // PREP_UNROLL, prepare, 0)

    def state_lhs(c, j):
        w = jnp.exp2(rows_t[c, ROW_END + j:ROW_END + j + 1, :]) * rows_t[c, ROW_DT + j:ROW_DT + j + 1, :]
        return (bt_s[c] * w).astype(BF16)

    def chunk_decay(c, pp, base):
        j = ROW_TOT + base + 2 * pp
        return jnp.where(head0, jnp.exp2(rows_t[c, j:j + 1, :]), jnp.exp2(rows_t[c, j + 1:j + 2, :]))

    s_b[...] = jnp.zeros_like(s_b)
    s_f[...] = jnp.zeros_like(s_f)

    def bwd_states(i, carry):
        chunks = [n_chunks - 1 - (i * SSD_UNROLL + k) for k in range(SSD_UNROLL)]
        local = []
        for c in chunks:
            rows = chunk_rows(c)
            lhs = [state_lhs(c, GROUP_HEADS + e) for e in range(GROUP_HEADS)]
            st = []
            for pp in range(GROUP_PAIRS):
                xsp = xs_c[rows, pair_cols(pp)].astype(BF16)
                st.append(jnp.where(head0, _dot(lhs[2 * pp], xsp), _dot(lhs[2 * pp + 1], xsp)))
            local.append(st)
        for k, c in enumerate(chunks):
            for pp in range(GROUP_PAIRS):
                sb_in[c, pp] = s_b[pp]
                s_b[pp] = s_b[pp] * chunk_decay(c, pp, GROUP_HEADS) + local[k][pp]
        return carry
    lax.fori_loop(0, n_chunks // SSD_UNROLL, bwd_states, 0)

    def outputs(i, carry):
        chunks = [i * SSD_UNROLL + k for k in range(SSD_UNROLL)]
        staged = []
        for c in chunks:
            rows = chunk_rows(c)
            c_b = c_c[rows, :].astype(BF16)
            cb = _dot(c_b, bt_s[c].astype(BF16))
            incl = col[c]
            lane_bcast = lambda j: jnp.broadcast_to(incl[:, j:j + 1], (t_len, t_len))
            src_row = lambda j: rows_t[c, ROW_SRC + j:ROW_SRC + j + 1, :]
            mix, col_f, col_b = [], [], []
            for e in range(GROUP_HEADS):
                jf, jb = e, GROUP_HEADS + e
                cf, cbk = lane_bcast(jf), lane_bcast(jb)
                w = jnp.exp2(jnp.where(causal, cf - src_row(jf), cbk - src_row(jb)))
                w = w + jnp.where(diag, rows_t[c, ROW_DT + jb:ROW_DT + jb + 1, :], 0.0)
                mix.append((cb * w).astype(BF16))
                col_f.append(cf)
                col_b.append(cbk)
            lhs_f = [state_lhs(c, e) for e in range(GROUP_HEADS)]
            per_pair = []
            for pp in range(GROUP_PAIRS):
                xs_f = xs_c[rows, pair_cols(pp)]
                xsp = xs_f.astype(BF16)
                y = jnp.where(head0, _dot(mix[2 * pp], xsp), _dot(mix[2 * pp + 1], xsp))
                y = y + (_dot(c_b, sb_in[c, pp].astype(BF16))
                         * jnp.exp2(jnp.where(head0, col_b[2 * pp], col_b[2 * pp + 1])))
                y = y + dsk_ref[:, pair_cols(pp)] * xs_f
                st = jnp.where(head0, _dot(lhs_f[2 * pp], xsp), _dot(lhs_f[2 * pp + 1], xsp))
                scale_f = jnp.exp2(jnp.where(head0, col_f[2 * pp], col_f[2 * pp + 1]))
                per_pair.append((y, st, scale_f))
            staged.append((c, rows, c_b, per_pair))
        for c, rows, c_b, per_pair in staged:
            sq = jnp.zeros((t_len, LANES), F32)
            for pp, (y, st, scale_f) in enumerate(per_pair):
                y = y + _dot(c_b, s_f[pp].astype(BF16)) * scale_f
                s_f[pp] = s_f[pp] * chunk_decay(c, pp, 0) + st
                yz = y * _silu(z_ref[rows, pair_cols(pp)])
                sq = sq + yz * yz
                o_ref[rows, pair_cols(pp)] = yz
            scale = lax.rsqrt(jnp.sum(sq, axis=-1, keepdims=True) * (1.0 / GROUP_WIDTH) + EPS)
            o_ref[rows, :] = o_ref[rows, :] * scale * ng_ref[...]
        return carry
    lax.fori_loop(0, n_chunks // SSD_UNROLL, outputs, 0)


def _ssd(xbc, z, dt, dt_bias, a_log, d_lanes, norm_g, batch, s_len):
    m = xbc.shape[0]
    g_of = lambda b, g: (b, g)
    b_off = SSD_INNER // LANES
    c_off = b_off + SSD_GROUPS
    n_chunks = s_len // SSD_CHUNK
    state = (GROUP_PAIRS, SSD_STATE, LANES)
    return pl.pallas_call(
        _ssd_body,
        grid=(batch, SSD_GROUPS),
        in_specs=[
            pl.BlockSpec((s_len, GROUP_WIDTH), g_of),
            pl.BlockSpec((s_len, LANES), lambda b, g: (b, b_off + g)),
            pl.BlockSpec((s_len, LANES), lambda b, g: (b, c_off + g)),
            pl.BlockSpec((s_len, GROUP_WIDTH), g_of),
            pl.BlockSpec((s_len, LANES), g_of),
            pl.BlockSpec((1, LANES), lambda b, g: (0, g)),
            pl.BlockSpec((1, LANES), lambda b, g: (0, g)),
            pl.BlockSpec((1, GROUP_WIDTH), lambda b, g: (0, g)),
            pl.BlockSpec((1, GROUP_WIDTH), lambda b, g: (0, g)),
        ],
        out_specs=pl.BlockSpec((s_len, GROUP_WIDTH), g_of),
        out_shape=jax.ShapeDtypeStruct((m, SSD_INNER), F32),
        scratch_shapes=[
            pltpu.VMEM((n_chunks,) + state, F32),
            pltpu.VMEM(state, F32),
            pltpu.VMEM(state, F32),
            pltpu.VMEM((n_chunks, SSD_CHUNK, LANES), F32),
            pltpu.VMEM((n_chunks, LANES, SSD_CHUNK), F32),
            pltpu.VMEM((n_chunks, SSD_STATE, SSD_CHUNK), F32),
        ],
        compiler_params=pltpu.CompilerParams(
            dimension_semantics=("parallel", "parallel"), vmem_limit_bytes=VMEM_LIMIT_BYTES),
        name="ssd",
    )(xbc, xbc, xbc, z, dt, dt_bias, a_log, d_lanes, norm_g)


def _outproj_body(s_len, a_ref, s_ref, sc_ref, scp_ref, scn_ref, x_ref, w_ref, cw_ref, cb_ref, o_ref, cx):
    tm = x_ref.shape[0]
    blocks_per_seq = s_len // tm
    i = pl.program_id(0) % blocks_per_seq
    w = SC_WIDTH
    gate_cx = lambda ref, rows: ref[rows, w:2 * w] * ref[rows, 2 * w:3 * w]
    cx[HALO:HALO + tm, :] = gate_cx(sc_ref, slice(None))
    cx[HALO - 1:HALO, :] = jnp.where(i == 0, 0.0, gate_cx(scp_ref, slice(HALO - 1, HALO)))
    cx[HALO + tm:HALO + tm + 1, :] = jnp.where(i == blocks_per_seq - 1, 0.0, gate_cx(scn_ref, slice(0, 1)))
    conv = cb_ref[...]
    for k in range(3):
        conv = conv + cw_ref[k:k + 1, :] * cx[HALO - 1 + k:HALO - 1 + k + tm, :]
    c_out = (sc_ref[:, 0:w] * conv).astype(BF16)
    acc = _dot(a_ref[...].astype(BF16), w_ref[0:ATTN_WIDTH, :])
    acc = acc + _dot(s_ref[...].astype(BF16), w_ref[ATTN_WIDTH:ATTN_WIDTH + SSD_INNER, :])
    acc = acc + _dot(c_out, w_ref[ATTN_WIDTH + SSD_INNER:, :])
    o_ref[...] = x_ref[...] + acc


def _out_proj(a, s, sc, x2, w, cw, cb, s_len):
    m = x2.shape[0]
    tm = OUT_TM
    halo_blocks = tm // HALO
    last_halo = m // HALO - 1
    rows = lambda n: pl.BlockSpec((tm, n), lambda i: (i, 0))
    return pl.pallas_call(
        functools.partial(_outproj_body, s_len),
        grid=(m // tm,),
        in_specs=[
            rows(ATTN_WIDTH), rows(SSD_INNER), rows(SC_W),
            pl.BlockSpec((HALO, SC_W), lambda i: (jnp.maximum(i * halo_blocks - 1, 0), 0)),
            pl.BlockSpec((HALO, SC_W), lambda i: (jnp.minimum((i + 1) * halo_blocks, last_halo), 0)),
            rows(D_MODEL),
            pl.BlockSpec(w.shape, lambda i: (0, 0), pipeline_mode=pl.Buffered(1)),
            pl.BlockSpec((3, SC_WIDTH), lambda i: (0, 0)),
            pl.BlockSpec((1, SC_WIDTH), lambda i: (0, 0)),
        ],
        out_specs=rows(D_MODEL),
        out_shape=jax.ShapeDtypeStruct((m, D_MODEL), F32),
        scratch_shapes=[pltpu.VMEM((tm + 2 * HALO, SC_WIDTH), F32)],
        compiler_params=pltpu.CompilerParams(
            dimension_semantics=("parallel",), vmem_limit_bytes=VMEM_LIMIT_BYTES),
        name="out_proj",
    )(a, s, sc, sc, sc, x2, w, cw, cb)


def _ffn_body(s_len, x_ref, xp_ref, xn_ref, g_ref, wu_ref, cw_ref, cb_ref, wd_ref, o_ref, h, ug, uv):
    tm = x_ref.shape[0]
    blocks_per_seq = s_len // tm
    i = pl.program_id(0) % blocks_per_seq

    def norm(x):
        ms = jnp.mean(x * x, axis=-1, keepdims=True)
        return x * lax.rsqrt(ms + EPS) * g_ref[...]

    h[0:tm, :] = norm(x_ref[...]).astype(BF16)
    halo = jnp.concatenate([jnp.where(i == 0, 0.0, norm(xp_ref[...])),
                            jnp.where(i == blocks_per_seq - 1, 0.0, norm(xn_ref[...]))], axis=0)
    h[tm:tm + 2 * HALO, :] = halo.astype(BF16)

    def up(col, width):
        return _dot(h[...], wu_ref[:, col:col + width])

    def conv3(u_scr, u, col, width):
        _store_with_halo(u_scr, u, tm)
        return jnp.concatenate(
            [_conv_taps(u_scr.at[s], cw_ref, cb_ref, slice(col + s * LANES, col + (s + 1) * LANES), 3, tm)
             for s in range(width // LANES)], axis=1)

    cols = [sum(FFN_TILES[:j]) for j in range(len(FFN_TILES))]
    pending = (up(cols[0], FFN_TILES[0]), up(D_FF + cols[0], FFN_TILES[0]))
    for j, width in enumerate(FFN_TILES):
        u_gate, u_val = pending
        if j + 1 < len(FFN_TILES):
            pending = (up(cols[j + 1], FFN_TILES[j + 1]), up(D_FF + cols[j + 1], FFN_TILES[j + 1]))
        gate = conv3(ug.at[j % 2], u_gate, cols[j], width)
        val = conv3(uv.at[j % 2], u_val, D_FF + cols[j], width)
        act = (_silu(gate) * val).astype(BF16)
        down = _dot(act, wd_ref[cols[j]:cols[j] + width, :])
        if j == 0:
            o_ref[...] = x_ref[...] + down
        else:
            o_ref[...] += down


def _ffn(x2, g, wu, cw, cb, wd, s_len):
    m = x2.shape[0]
    tm = FFN_TM
    halo_blocks = tm // HALO
    last_halo = m // HALO - 1
    const = lambda a: pl.BlockSpec(a.shape, lambda i: (0, 0), pipeline_mode=pl.Buffered(1))
    return pl.pallas_call(
        functools.partial(_ffn_body, s_len),
        grid=(m // tm,),
        in_specs=[
            pl.BlockSpec((tm, D_MODEL), lambda i: (i, 0)),
            pl.BlockSpec((HALO, D_MODEL), lambda i: (jnp.maximum(i * halo_blocks - 1, 0), 0)),
            pl.BlockSpec((HALO, D_MODEL), lambda i: (jnp.minimum((i + 1) * halo_blocks, last_halo), 0)),
            pl.BlockSpec((1, D_MODEL), lambda i: (0, 0)),
            const(wu), const(cw), const(cb), const(wd),
        ],
        out_specs=pl.BlockSpec((tm, D_MODEL), lambda i: (i, 0)),
        out_shape=jax.ShapeDtypeStruct((m, D_MODEL), F32),
        scratch_shapes=[
            pltpu.VMEM((tm + 2 * HALO, D_MODEL), BF16),
            pltpu.VMEM((2, max(FFN_TILES) // LANES, tm + 2 * HALO, LANES), F32),
            pltpu.VMEM((2, max(FFN_TILES) // LANES, tm + 2 * HALO, LANES), F32),
        ],
        compiler_params=pltpu.CompilerParams(
            dimension_semantics=("parallel",), vmem_limit_bytes=VMEM_LIMIT_BYTES),
        name="conv_ffn",
    )(x2, x2, x2, g, wu, cw, cb, wd)


def _regroup_dt(v):
    lead = v.shape[:-1]
    v = v.reshape(lead + (2, SSD_GROUPS, GROUP_HEADS))
    v = jnp.moveaxis(v, -3, -2).reshape(lead + (SSD_GROUPS, DT_SLOTS))
    v = jnp.tile(v, DT_COPIES)
    v = jnp.pad(v, [(0, 0)] * len(lead) + [(0, 0), (0, LANES - DT_COPIES * DT_SLOTS)])
    return v.reshape(lead + (DT_W,))


def _prep_w_in(w):
    head = 3 * ATTN_WIDTH + SSD_INNER + SSD_XBC
    dt_cols = _regroup_dt(w[:, head:head + 2 * SSD_HEADS])
    return jnp.concatenate([w[:, :head], dt_cols, w[:, head + 2 * SSD_HEADS:]], axis=1).astype(BF16)


def kernel(x, rel_table, norm1_g, w_in, q_norm_g, k_norm_g, ssd_conv_w, ssd_conv_b, ssd_dt_bias, ssd_a_log, ssd_d,
           ssd_norm_g, sc_conv_w, sc_conv_b, w_out, norm2_g, ffn_w_up, ffn_conv_w, ffn_conv_b, ffn_w_down):
    batch, s_len, d_model = x.shape
    assert d_model == D_MODEL and s_len == DILATED_PATTERNS[-1][0]
    assert s_len % OUT_TM == 0 and s_len % FFN_TM == 0 and s_len % IN_TM == 0
    depth = w_in.shape[0]
    x2 = x.reshape(batch * s_len, d_model)
    bias_wide, bias_narrow = _bias_tables(rel_table)
    row = lambda v: v.reshape(1, -1).astype(F32)
    for i in range(depth):
        qkv, z, xbc, dt, sc = _in_proj(x2, row(norm1_g[i]), _prep_w_in(w_in[i]), ssd_conv_w[i],
                                       row(ssd_conv_b[i]), s_len)
        gq = row(jnp.tile(q_norm_g[i], 2)) * (HEAD_DIM ** -0.5)
        gk = row(jnp.tile(k_norm_g[i], 2))
        a_out = _attention(qkv, gq, gk, bias_wide, bias_narrow, batch, s_len)
        s_out = _ssd(xbc, z, dt,
                     row(_regroup_dt(ssd_dt_bias[i].reshape(-1))), row(_regroup_dt(ssd_a_log[i].reshape(-1))),
                     row(jnp.repeat(ssd_d[i], SSD_HEADDIM)), row(ssd_norm_g[i]), batch, s_len)
        x2 = _out_proj(a_out, s_out, sc, x2, w_out[i].astype(BF16), sc_conv_w[i], row(sc_conv_b[i]), s_len)
        x2 = _ffn(x2, row(norm2_g[i]), ffn_w_up[i].astype(BF16), ffn_conv_w[i], row(ffn_conv_b[i]),
                  ffn_w_down[i].astype(BF16), s_len)
    return x2.reshape(batch, s_len, d_model)
```

```python
import functools
import math

import jax
import jax.numpy as jnp
from jax import lax
from jax.experimental import pallas as pl
from jax.experimental.pallas import tpu as pltpu

F32 = jnp.float32
BF16 = jnp.bfloat16

LANES = 128
VMEM_LIMIT_BYTES = 56 * 1024 * 1024

D_MODEL = 1024
ATTN_HEADS = 8
HEAD_DIM = 64
ATTN_WIDTH = ATTN_HEADS * HEAD_DIM
DILATED_PATTERNS = ((128, 1), (512, 4), (2048, 16))
BAND = 64
REL_BUCKETS = 32
REL_MAX_DISTANCE = 1024
SSD_HEADS = 16
SSD_HEADDIM = 64
SSD_INNER = SSD_HEADS * SSD_HEADDIM
SSD_GROUPS = 2
SSD_STATE = 128
SSD_XBC = SSD_INNER + 2 * SSD_GROUPS * SSD_STATE
SSD_CONV = 7
SSD_CHUNK = 128
GROUP_HEADS = SSD_HEADS // SSD_GROUPS
GROUP_WIDTH = SSD_INNER // SSD_GROUPS
GROUP_PAIRS = GROUP_WIDTH // LANES
SC_WIDTH = 512
D_FF = 2816
EPS = 1e-6
NEG_INF = -1e30
LOG2_E = math.log2(math.e)

QKV_W = 3 * ATTN_WIDTH
DT_W = SSD_GROUPS * LANES
SC_W = 3 * SC_WIDTH
IN_PIECES = (QKV_W, SSD_INNER, SSD_XBC, DT_W, SC_W)

IN_TM = 256
OUT_TM = 512
FFN_TM = 512
FFN_TILES = (1536, 1280)
assert sum(FFN_TILES) == D_FF
HALO = 8
QBLK = 128
KWIN = 2 * QBLK
ATTN_UNROLL = 16
SSD_UNROLL = 4
PREP_UNROLL = 8
DT_SLOTS = 2 * GROUP_HEADS
DT_COPIES = 4
ROW_SRC, ROW_END, ROW_DT, ROW_TOT = (q * DT_SLOTS for q in range(DT_COPIES))


def _split3(x):
    p0 = x.astype(BF16)
    r = x - p0.astype(F32)
    p1 = r.astype(BF16)
    p2 = (r - p1.astype(F32)).astype(BF16)
    return p0, p1, p2


def _dot(a, b):
    return jnp.dot(a, b, preferred_element_type=F32)


def _dot_nt(a, b):
    return lax.dot_general(a, b, (((1,), (1,)), ((), ())), preferred_element_type=F32)


def _silu(x):
    return x / (1.0 + jnp.exp(-x))


def _store_with_halo(u, v, tm):
    for s in range(v.shape[1] // LANES):
        cols = slice(s * LANES, (s + 1) * LANES)
        u[s, HALO:HALO + tm, :] = v[0:tm, cols]
        u[s, 0:HALO, :] = v[tm:tm + HALO, cols]
        u[s, HALO + tm:2 * HALO + tm, :] = v[tm + HALO:tm + 2 * HALO, cols]


def _conv_taps(u_slab, w_ref, b_ref, cols, taps, tm):
    lead = HALO - taps // 2
    acc = b_ref[:, cols]
    for k in range(taps):
        acc = acc + w_ref[k:k + 1, cols] * u_slab[lead + k:lead + k + tm, :]
    return acc


def _inproj_body(s_len, x_ref, xp_ref, xn_ref, g_ref, w_ref, cw_ref, cb_ref,
                 qkv_ref, z_ref, xbc_ref, dt_ref, sc_ref, h, u):
    tm = x_ref.shape[0]
    blocks_per_seq = s_len // tm
    i = pl.program_id(0) % blocks_per_seq

    def norm(x):
        ms = jnp.mean(x * x, axis=-1, keepdims=True)
        return x * lax.rsqrt(ms + EPS) * g_ref[...]

    h[0:tm, :] = norm(x_ref[...]).astype(BF16)
    halo = jnp.concatenate([jnp.where(i == 0, 0.0, norm(xp_ref[...])),
                            jnp.where(i == blocks_per_seq - 1, 0.0, norm(xn_ref[...]))], axis=0)
    h[tm:tm + 2 * HALO, :] = halo.astype(BF16)

    xbc_off = QKV_W + SSD_INNER
    pre = _dot(h[...], w_ref[:, xbc_off:xbc_off + SSD_XBC])
    _store_with_halo(u, pre, tm)
    hb = h[0:tm, :]
    off = 0
    for ref in (qkv_ref, z_ref, None, dt_ref, sc_ref):
        n = SSD_XBC if ref is None else ref.shape[-1]
        if ref is not None:
            ref[...] = _dot(hb, w_ref[:, off:off + n])
        off += n
    for s in range(SSD_XBC // LANES):
        cols = slice(s * LANES, (s + 1) * LANES)
        xbc_ref[:, cols] = _silu(_conv_taps(u.at[s], cw_ref, cb_ref, cols, SSD_CONV, tm))


def _in_proj(x2, g, w, conv_w, conv_b, s_len):
    m = x2.shape[0]
    n_total = sum(IN_PIECES)
    halo_blocks = IN_TM // HALO
    last_halo = m // HALO - 1
    return pl.pallas_call(
        functools.partial(_inproj_body, s_len),
        grid=(m // IN_TM,),
        in_specs=[
            pl.BlockSpec((IN_TM, D_MODEL), lambda i: (i, 0)),
            pl.BlockSpec((HALO, D_MODEL), lambda i: (jnp.maximum(i * halo_blocks - 1, 0), 0)),
            pl.BlockSpec((HALO, D_MODEL), lambda i: (jnp.minimum((i + 1) * halo_blocks, last_halo), 0)),
            pl.BlockSpec((1, D_MODEL), lambda i: (0, 0)),
            pl.BlockSpec((D_MODEL, n_total), lambda i: (0, 0), pipeline_mode=pl.Buffered(1)),
            pl.BlockSpec((SSD_CONV, SSD_XBC), lambda i: (0, 0)),
            pl.BlockSpec((1, SSD_XBC), lambda i: (0, 0)),
        ],
        out_specs=[pl.BlockSpec((IN_TM, n), lambda i: (i, 0)) for n in IN_PIECES],
        out_shape=[jax.ShapeDtypeStruct((m, n), F32) for n in IN_PIECES],
        scratch_shapes=[
            pltpu.VMEM((IN_TM + 2 * HALO, D_MODEL), BF16),
            pltpu.VMEM((SSD_XBC // LANES, IN_TM + 2 * HALO, LANES), F32),
        ],
        compiler_params=pltpu.CompilerParams(
            dimension_semantics=("parallel",), vmem_limit_bytes=VMEM_LIMIT_BYTES),
        name="in_proj",
    )(x2, x2, x2, g, w, conv_w, conv_b)


def _attn_body(q_ref, k_ref, v_ref, gq_ref, gk_ref, bw_ref, bn_ref, o_ref,
               qn, kn, q4, k4, v4, qp0, qp1, kp, vp, res_a, res_b):
    s_len = q_ref.shape[0]
    (_, d_fine), (_, d_mid), (_, d_coarse) = DILATED_PATTERNS
    ratio = d_mid // d_fine
    assert d_fine == 1 and d_coarse == ratio * d_mid
    sub_mid = s_len // d_mid
    sub_coarse = s_len // d_coarse
    assert sub_coarse == QBLK
    lane = lax.broadcasted_iota(jnp.int32, (1, LANES), 1)
    head0 = lane < HEAD_DIM
    rr = lax.broadcasted_iota(jnp.int32, (LANES, LANES), 0)
    cc = lax.broadcasted_iota(jnp.int32, (LANES, LANES), 1)
    same_head = jnp.where((rr < HEAD_DIM) == (cc < HEAD_DIM), 1.0, 0.0).astype(BF16)
    chunk = 256

    for c0 in range(0, s_len, chunk):
        rows = pl.ds(c0, chunk)
        for src, g_ref, dst in ((q_ref, gq_ref, qn), (k_ref, gk_ref, kn)):
            x = src[rows, :]
            sq = x * x
            hi = sq.astype(BF16)
            lo = (sq - hi.astype(F32)).astype(BF16)
            ss = _dot(hi, same_head) + _dot(lo, same_head)
            dst[rows, :] = x * lax.rsqrt(ss * (1.0 / HEAD_DIM) + EPS) * g_ref[...]

    for r in range(ratio):
        for c0 in range(0, sub_mid, chunk):
            src = pl.ds(r + c0 * ratio, chunk, stride=ratio)
            dst = pl.ds(r * sub_mid + c0, chunk)
            q4[dst, :] = qn[src, :]
            k4[dst, :] = kn[src, :]
            v4[dst, :] = v_ref[src, :]

    def write_operands(dst, qv, kv, vv):
        qp0[dst, :] = jnp.where(head0, qv, 0.0).astype(BF16)
        qp1[dst, :] = jnp.where(head0, 0.0, qv).astype(BF16)
        kp[dst, :] = kv.astype(BF16)
        vp[dst, :] = vv.astype(BF16)

    def softmax_units(res, specs):
        scores = []
        for qrows, krows, bias_of_head in specs:
            kw = kp[krows, :]
            for h, qp in enumerate((qp0, qp1)):
                scores.append(_dot_nt(qp[qrows, :], kw) + bias_of_head(h))
        probs = []
        for s in scores:
            m = jnp.max(s, axis=-1, keepdims=True)
            p = jnp.exp(s - m)
            probs.append((p.astype(BF16), m, jnp.sum(p, axis=-1, keepdims=True)))
        for i, (qrows, krows, _) in enumerate(specs):
            vw = vp[krows, :]
            (p0, m0, l0), (p1, m1, l1) = probs[2 * i], probs[2 * i + 1]
            res[0, qrows, :] = jnp.where(head0, _dot(p0, vw), _dot(p1, vw))
            res[1, qrows, :] = jnp.where(head0, m0, m1)
            res[2, qrows, :] = jnp.where(head0, l0, l1)

    def attend_wide(res, pi, sub):
        nb = sub // QBLK

        def units(i, carry):
            specs = []
            for j in range(ATTN_UNROLL):
                u = i * ATTN_UNROLL + j
                t = u % nb
                qrows = pl.ds(pl.multiple_of(u * QBLK, QBLK), QBLK)
                kstart = (u - t) * QBLK + jnp.clip(t * QBLK - BAND, 0, sub - KWIN)
                krows = pl.ds(pl.multiple_of(kstart, BAND), KWIN)
                var = jnp.where(t == 0, 0, jnp.where(t == nb - 1, 2, 1))
                specs.append((qrows, krows, lambda h, var=var: bw_ref[pi, var, h]))
            softmax_units(res, specs)
            return carry
        lax.fori_loop(0, s_len // QBLK // ATTN_UNROLL, units, 0)

    def attend_narrow(res):
        def units(i, carry):
            specs = []
            for j in range(ATTN_UNROLL):
                rows = pl.ds(pl.multiple_of((i * ATTN_UNROLL + j) * QBLK, QBLK), QBLK)
                specs.append((rows, rows, lambda h: bn_ref[h]))
            softmax_units(res, specs)
            return carry
        lax.fori_loop(0, s_len // QBLK // ATTN_UNROLL, units, 0)

    def merged(a0, m0, l0, a1, m1, l1):
        mn = jnp.maximum(m0, m1)
        e0 = jnp.exp(m0 - mn)
        e1 = jnp.exp(m1 - mn)
        return a0 * e0 + a1 * e1, mn, l0 * e0 + l1 * e1

    for r in range(ratio):
        for r2 in range(ratio):
            src = pl.ds(r * sub_mid + r2, sub_coarse, stride=ratio)
            dst = pl.ds((r + ratio * r2) * sub_coarse, sub_coarse)
            write_operands(dst, q4[src, :], k4[src, :], v4[src, :])
    attend_narrow(res_a)

    for c0 in range(0, s_len, chunk):
        rows = pl.ds(c0, chunk)
        write_operands(rows, q4[rows, :], k4[rows, :], v4[rows, :])
    attend_wide(res_b, 1, sub_mid)
    for r in range(ratio):
        for r2 in range(ratio):
            fine = pl.ds(r * sub_mid + r2, sub_coarse, stride=ratio)
            coarse = pl.ds((r + ratio * r2) * sub_coarse, sub_coarse)
            a, m, l = merged(res_b[0, fine, :], res_b[1, fine, :], res_b[2, fine, :],
                             res_a[0, coarse, :], res_a[1, coarse, :], res_a[2, coarse, :])
            res_b[0, fine, :] = a
            res_b[1, fine, :] = m
            res_b[2, fine, :] = l

    for c0 in range(0, s_len, chunk):
        rows = pl.ds(c0, chunk)
        write_operands(rows, qn[rows, :], kn[rows, :], v_ref[rows, :])
    attend_wide(res_a, 0, s_len)
    for r in range(ratio):
        for c0 in range(0, sub_mid, chunk):
            nat = pl.ds(r + c0 * ratio, chunk, stride=ratio)
            mid = pl.ds(r * sub_mid + c0, chunk)
            a, _, l = merged(res_a[0, nat, :], res_a[1, nat, :], res_a[2, nat, :],
                             res_b[0, mid, :], res_b[1, mid, :], res_b[2, mid, :])
            o_ref[nat, :] = a / l


def _attention(qkv, gq, gk, bias_wide, bias_narrow, batch, s_len):
    m = qkv.shape[0]
    pairs = ATTN_WIDTH // LANES
    slab = lambda off: pl.BlockSpec((s_len, LANES), lambda b, p: (b, off + p))
    row = pl.BlockSpec((1, LANES), lambda b, p: (0, 0))
    f32_slab = pltpu.VMEM((s_len, LANES), F32)
    bf16_slab = pltpu.VMEM((s_len, LANES), BF16)
    stats = pltpu.VMEM((3, s_len, LANES), F32)
    return pl.pallas_call(
        _attn_body,
        grid=(batch, pairs),
        in_specs=[
            slab(0), slab(pairs), slab(2 * pairs), row, row,
            pl.BlockSpec((2, 3, 2, QBLK, KWIN), lambda b, p: (0, 0, p, 0, 0)),
            pl.BlockSpec((2, QBLK, QBLK), lambda b, p: (p, 0, 0)),
        ],
        out_specs=pl.BlockSpec((s_len, LANES), lambda b, p: (b, p)),
        out_shape=jax.ShapeDtypeStruct((m, ATTN_WIDTH), F32),
        scratch_shapes=[f32_slab, f32_slab, f32_slab, f32_slab, f32_slab,
                        bf16_slab, bf16_slab, bf16_slab, bf16_slab, stats, stats],
        compiler_params=pltpu.CompilerParams(
            dimension_semantics=("parallel", "parallel"), vmem_limit_bytes=VMEM_LIMIT_BYTES),
        name="dilated_attention",
    )(qkv, qkv, qkv, gq, gk, bias_wide, bias_narrow)


def _t5_bucket(rel):
    nb = REL_BUCKETS // 2
    max_exact = nb // 2
    ret = jnp.where(rel > 0, nb, 0)
    n = jnp.abs(rel)
    nf = jnp.maximum(n, 1).astype(F32)
    large = max_exact + (jnp.log(nf / max_exact) / math.log(REL_MAX_DISTANCE / max_exact)
                         * (nb - max_exact)).astype(jnp.int32)
    large = jnp.minimum(large, nb - 1)
    return ret + jnp.where(n < max_exact, n, large)


def _bias_tables(rel_table):
    period = 2 * KWIN

    def table(d, n_keys, offset):
        k = jnp.arange(period)
        delta = jnp.where(k < KWIN, k, k - period) - offset
        g = rel_table[_t5_bucket(delta * d)].astype(F32)
        g = jnp.where((jnp.abs(delta) <= BAND)[:, None], g, NEG_INF).T
        flat = jnp.tile(g, (1, QBLK))[:, :QBLK * (period - 1)]
        return flat.reshape(ATTN_HEADS, QBLK, period - 1)[:, :, :n_keys]
    wide = jnp.stack([jnp.stack([table(d, KWIN, off) for off in (0, BAND, 2 * BAND)])
                      for (_, d) in DILATED_PATTERNS[:2]])
    narrow = table(DILATED_PATTERNS[2][1], QBLK, 0)
    return wide, narrow


def _ssd_body(xs_c, b_c, c_c, z_ref, dt_ref, dtb_ref, alog_ref, dsk_ref, ng_ref, o_ref,
              sb_in, s_f, s_b, col, rows_t, bt_s, yz_s):
    s_len = z_ref.shape[0]
    t_len = SSD_CHUNK
    n_chunks = s_len // t_len
    lane = lax.broadcasted_iota(jnp.int32, (1, LANES), 1)
    head0 = lane < SSD_HEADDIM
    fwd_slot = lane % DT_SLOTS < GROUP_HEADS
    copy = lane // DT_SLOTS
    rr = lax.broadcasted_iota(jnp.int32, (t_len, t_len), 0)
    cc = lax.broadcasted_iota(jnp.int32, (t_len, t_len), 1)
    tri_le = jnp.where(cc <= rr, 1.0, 0.0).astype(BF16)
    causal = cc <= rr
    diag = cc == rr
    a_row = -jnp.exp(alog_ref[...]) * LOG2_E

    chunk_rows = lambda c: pl.ds(pl.multiple_of(c * t_len, t_len), t_len)
    pair_cols = lambda pp: slice(pp * LANES, (pp + 1) * LANES)

    def prepare(i, carry):
        for k in range(PREP_UNROLL):
            c = i * PREP_UNROLL + k
            rows = chunk_rows(c)
            v = dt_ref[rows, :] + dtb_ref[...]
            dt_c = jnp.maximum(v, 0.0) + jnp.log1p(jnp.exp(-jnp.abs(v)))
            dta = dt_c * a_row
            p0, p1, p2 = _split3(dta)
            fwd_incl = _dot(tri_le, p0) + _dot(tri_le, p1) + _dot(tri_le, p2)
            total = fwd_incl[t_len - 1:t_len, :]
            incl = jnp.where(fwd_slot, fwd_incl, total - fwd_incl + dta)
            to_edge = jnp.where(fwd_slot, total - fwd_incl, fwd_incl - dta)
            col[c] = incl
            packed = jnp.where(copy == 0, incl - jnp.log2(dt_c),
                               jnp.where(copy == 1, to_edge, jnp.where(copy == 2, dt_c, total)))
            rows_t[c] = packed.T
            bt_s[c] = b_c[rows, :].T
        return carry
    lax.fori_loop(0, n_chunks // PREP_UNROLL, prepare, 0)

    def state_lhs(c, j):
        w = jnp.exp2(rows_t[c, ROW_END + j:ROW_END + j + 1, :]) * rows_t[c, ROW_DT + j:ROW_DT + j + 1, :]
        return (bt_s[c] * w).astype(BF16)

    def chunk_decay(c, pp, base):
        j = ROW_TOT + base + 2 * pp
        return jnp.where(head0, jnp.exp2(rows_t[c, j:j + 1, :]), jnp.exp2(rows_t[c, j + 1:j + 2, :]))

    s_b[...] = jnp.zeros_like(s_b)
    s_f[...] = jnp.zeros_like(s_f)

    def bwd_states(i, carry):
        chunks = [n_chunks - 1 - (i * SSD_UNROLL + k) for k in range(SSD_UNROLL)]
        local = []
        for c in chunks:
            rows = chunk_rows(c)
            lhs = [state_lhs(c, GROUP_HEADS + e) for e in range(GROUP_HEADS)]
            st = []
            for pp in range(GROUP_PAIRS):
                xsp = xs_c[rows, pair_cols(pp)].astype(BF16)
                st.append(jnp.where(head0, _dot(lhs[2 * pp], xsp), _dot(lhs[2 * pp + 1], xsp)))
            local.append(st)
        for k, c in enumerate(chunks):
            for pp in range(GROUP_PAIRS):
                sb_in[c, pp] = s_b[pp]
                s_b[pp] = s_b[pp] * chunk_decay(c, pp, GROUP_HEADS) + local[k][pp]
        return carry
    lax.fori_loop(0, n_chunks // SSD_UNROLL, bwd_states, 0)

    def outputs(i, carry):
        chunks = [i * SSD_UNROLL + k for k in range(SSD_UNROLL)]
        staged = []
        for c in chunks:
            rows = chunk_rows(c)
            c_b = c_c[rows, :].astype(BF16)
            cb = _dot(c_b, bt_s[c].astype(BF16))
            incl = col[c]
            lane_bcast = lambda j: jnp.broadcast_to(incl[:, j:j + 1], (t_len, t_len))
            src_row = lambda j: rows_t[c, ROW_SRC + j:ROW_SRC + j + 1, :]
            mix, col_f, col_b = [], [], []
            for e in range(GROUP_HEADS):
                jf, jb = e, GROUP_HEADS + e
                cf, cbk = lane_bcast(jf), lane_bcast(jb)
                w = jnp.exp2(jnp.where(causal, cf - src_row(jf), cbk - src_row(jb)))
                w = w + jnp.where(diag, rows_t[c, ROW_DT + jb:ROW_DT + jb + 1, :], 0.0)
                mix.append((cb * w).astype(BF16))
                col_f.append(cf)
                col_b.append(cbk)
            lhs_f = [state_lhs(c, e) for e in range(GROUP_HEADS)]
            per_pair = []
            for pp in range(GROUP_PAIRS):
                xs_f = xs_c[rows, pair_cols(pp)]
                xsp = xs_f.astype(BF16)
                y = jnp.where(head0, _dot(mix[2 * pp], xsp), _dot(mix[2 * pp + 1], xsp))
                y = y + (_dot(c_b, sb_in[c, pp].astype(BF16))
                         * jnp.exp2(jnp.where(head0, col_b[2 * pp], col_b[2 * pp + 1])))
                y = y + dsk_ref[:, pair_cols(pp)] * xs_f
                st = jnp.where(head0, _dot(lhs_f[2 * pp], xsp), _dot(lhs_f[2 * pp + 1], xsp))
                scale_f = jnp.exp2(jnp.where(head0, col_f[2 * pp], col_f[2 * pp + 1]))
                per_pair.append((y, st, scale_f))
            staged.append((c, rows, c_b, per_pair))
        for c, rows, c_b, per_pair in staged:
            sq = jnp.zeros((t_len, LANES), F32)
            for pp, (y, st, scale_f) in enumerate(per_pair):
                y = y + _dot(c_b, s_f[pp].astype(BF16)) * scale_f
                s_f[pp] = s_f[pp] * chunk_decay(c, pp, 0) + st
                yz = y * _silu(z_ref[rows, pair_cols(pp)])
                sq = sq + yz * yz
                yz_s[:, pair_cols(pp)] = yz
            scale = lax.rsqrt(jnp.sum(sq, axis=-1, keepdims=True) * (1.0 / GROUP_WIDTH) + EPS)
            o_ref[rows, :] = (yz_s[...] * scale * ng_ref[...]).astype(BF16)
        return carry
    lax.fori_loop(0, n_chunks // SSD_UNROLL, outputs, 0)


def _ssd(xbc, z, dt, dt_bias, a_log, d_lanes, norm_g, batch, s_len):
    m = xbc.shape[0]
    g_of = lambda b, g: (b, g)
    b_off = SSD_INNER // LANES
    c_off = b_off + SSD_GROUPS
    n_chunks = s_len // SSD_CHUNK
    state = (GROUP_PAIRS, SSD_STATE, LANES)
    return pl.pallas_call(
        _ssd_body,
        grid=(batch, SSD_GROUPS),
        in_specs=[
            pl.BlockSpec((s_len, GROUP_WIDTH), g_of),
            pl.BlockSpec((s_len, LANES), lambda b, g: (b, b_off + g)),
            pl.BlockSpec((s_len, LANES), lambda b, g: (b, c_off + g)),
            pl.BlockSpec((s_len, GROUP_WIDTH), g_of),
            pl.BlockSpec((s_len, LANES), g_of),
            pl.BlockSpec((1, LANES), lambda b, g: (0, g)),
            pl.BlockSpec((1, LANES), lambda b, g: (0, g)),
            pl.BlockSpec((1, GROUP_WIDTH), lambda b, g: (0, g)),
            pl.BlockSpec((1, GROUP_WIDTH), lambda b, g: (0, g)),
        ],
        out_specs=pl.BlockSpec((s_len, GROUP_WIDTH), g_of),
        out_shape=jax.ShapeDtypeStruct((m, SSD_INNER), BF16),
        scratch_shapes=[
            pltpu.VMEM((n_chunks,) + state, F32),
            pltpu.VMEM(state, F32),
            pltpu.VMEM(state, F32),
            pltpu.VMEM((n_chunks, SSD_CHUNK, LANES), F32),
            pltpu.VMEM((n_chunks, LANES, SSD_CHUNK), F32),
            pltpu.VMEM((n_chunks, SSD_STATE, SSD_CHUNK), F32),
            pltpu.VMEM((SSD_CHUNK, GROUP_WIDTH), F32),
        ],
        compiler_params=pltpu.CompilerParams(
            dimension_semantics=("parallel", "parallel"), vmem_limit_bytes=VMEM_LIMIT_BYTES),
        name="ssd",
    )(xbc, xbc, xbc, z, dt, dt_bias, a_log, d_lanes, norm_g)


def _outproj_body(s_len, a_ref, s_ref, sc_ref, scp_ref, scn_ref, x_ref, w_ref, cw_ref, cb_ref, o_ref, cx):
    tm = x_ref.shape[0]
    blocks_per_seq = s_len // tm
    i = pl.program_id(0) % blocks_per_seq
    w = SC_WIDTH
    gate_cx = lambda ref, rows: ref[rows, w:2 * w] * ref[rows, 2 * w:3 * w]
    cx[HALO:HALO + tm, :] = gate_cx(sc_ref, slice(None))
    cx[HALO - 1:HALO, :] = jnp.where(i == 0, 0.0, gate_cx(scp_ref, slice(HALO - 1, HALO)))
    cx[HALO + tm:HALO + tm + 1, :] = jnp.where(i == blocks_per_seq - 1, 0.0, gate_cx(scn_ref, slice(0, 1)))
    conv = cb_ref[...]
    for k in range(3):
        conv = conv + cw_ref[k:k + 1, :] * cx[HALO - 1 + k:HALO - 1 + k + tm, :]
    c_out = (sc_ref[:, 0:w] * conv).astype(BF16)
    acc = _dot(a_ref[...].astype(BF16), w_ref[0:ATTN_WIDTH, :])
    acc = acc + _dot(s_ref[...].astype(BF16), w_ref[ATTN_WIDTH:ATTN_WIDTH + SSD_INNER, :])
    acc = acc + _dot(c_out, w_ref[ATTN_WIDTH + SSD_INNER:, :])
    o_ref[...] = x_ref[...] + acc


def _out_proj(a, s, sc, x2, w, cw, cb, s_len):
    m = x2.shape[0]
    tm = OUT_TM
    halo_blocks = tm // HALO
    last_halo = m // HALO - 1
    rows = lambda n: pl.BlockSpec((tm, n), lambda i: (i, 0))
    return pl.pallas_call(
        functools.partial(_outproj_body, s_len),
        grid=(m // tm,),
        in_specs=[
            rows(ATTN_WIDTH), rows(SSD_INNER), rows(SC_W),
            pl.BlockSpec((HALO, SC_W), lambda i: (jnp.maximum(i * halo_blocks - 1, 0), 0)),
            pl.BlockSpec((HALO, SC_W), lambda i: (jnp.minimum((i + 1) * halo_blocks, last_halo), 0)),
            rows(D_MODEL),
            pl.BlockSpec(w.shape, lambda i: (0, 0), pipeline_mode=pl.Buffered(1)),
            pl.BlockSpec((3, SC_WIDTH), lambda i: (0, 0)),
            pl.BlockSpec((1, SC_WIDTH), lambda i: (0, 0)),
        ],
        out_specs=rows(D_MODEL),
        out_shape=jax.ShapeDtypeStruct((m, D_MODEL), F32),
        scratch_shapes=[pltpu.VMEM((tm + 2 * HALO, SC_WIDTH), F32)],
        compiler_params=pltpu.CompilerParams(
            dimension_semantics=("parallel",), vmem_limit_bytes=VMEM_LIMIT_BYTES),
        name="out_proj",
    )(a, s, sc, sc, sc, x2, w, cw, cb)


def _ffn_body(s_len, x_ref, xp_ref, xn_ref, g_ref, wu_ref, cw_ref, cb_ref, wd_ref, o_ref, h, ug, uv):
    tm = x_ref.shape[0]
    blocks_per_seq = s_len // tm
    i = pl.program_id(0) % blocks_per_seq

    def norm(x):
        ms = jnp.mean(x * x, axis=-1, keepdims=True)
        return x * lax.rsqrt(ms + EPS) * g_ref[...]

    h[0:tm, :] = norm(x_ref[...]).astype(BF16)
    halo = jnp.concatenate([jnp.where(i == 0, 0.0, norm(xp_ref[...])),
                            jnp.where(i == blocks_per_seq - 1, 0.0, norm(xn_ref[...]))], axis=0)
    h[tm:tm + 2 * HALO, :] = halo.astype(BF16)

    def up(col, width):
        return _dot(h[...], wu_ref[:, col:col + width])

    def conv3(u_scr, u, col, width):
        _store_with_halo(u_scr, u, tm)
        return jnp.concatenate(
            [_conv_taps(u_scr.at[s], cw_ref, cb_ref, slice(col + s * LANES, col + (s + 1) * LANES), 3, tm)
             for s in range(width // LANES)], axis=1)

    cols = [sum(FFN_TILES[:j]) for j in range(len(FFN_TILES))]
    pending = (up(cols[0], FFN_TILES[0]), up(D_FF + cols[0], FFN_TILES[0]))
    for j, width in enumerate(FFN_TILES):
        u_gate, u_val = pending
        if j + 1 < len(FFN_TILES):
            pending = (up(cols[j + 1], FFN_TILES[j + 1]), up(D_FF + cols[j + 1], FFN_TILES[j + 1]))
        gate = conv3(ug.at[j % 2], u_gate, cols[j], width)
        val = conv3(uv.at[j % 2], u_val, D_FF + cols[j], width)
        act = (_silu(gate) * val).astype(BF16)
        down = _dot(act, wd_ref[cols[j]:cols[j] + width, :])
        if j == 0:
            o_ref[...] = x_ref[...] + down
        else:
            o_ref[...] += down


def _ffn(x2, g, wu, cw, cb, wd, s_len):
    m = x2.shape[0]
    tm = FFN_TM
    halo_blocks = tm // HALO
    last_halo = m // HALO - 1
    const = lambda a: pl.BlockSpec(a.shape, lambda i: (0, 0), pipeline_mode=pl.Buffered(1))
    return pl.pallas_call(
        functools.partial(_ffn_body, s_len),
        grid=(m // tm,),
        in_specs=[
            pl.BlockSpec((tm, D_MODEL), lambda i: (i, 0)),
            pl.BlockSpec((HALO, D_MODEL), lambda i: (jnp.maximum(i * halo_blocks - 1, 0), 0)),
            pl.BlockSpec((HALO, D_MODEL), lambda i: (jnp.minimum((i + 1) * halo_blocks, last_halo), 0)),
            pl.BlockSpec((1, D_MODEL), lambda i: (0, 0)),
            const(wu), const(cw), const(cb), const(wd),
        ],
        out_specs=pl.BlockSpec((tm, D_MODEL), lambda i: (i, 0)),
        out_shape=jax.ShapeDtypeStruct((m, D_MODEL), F32),
        scratch_shapes=[
            pltpu.VMEM((tm + 2 * HALO, D_MODEL), BF16),
            pltpu.VMEM((2, max(FFN_TILES) // LANES, tm + 2 * HALO, LANES), F32),
            pltpu.VMEM((2, max(FFN_TILES) // LANES, tm + 2 * HALO, LANES), F32),
        ],
        compiler_params=pltpu.CompilerParams(
            dimension_semantics=("parallel",), vmem_limit_bytes=VMEM_LIMIT_BYTES),
        name="conv_ffn",
    )(x2, x2, x2, g, wu, cw, cb, wd)


def _regroup_dt(v):
    lead = v.shape[:-1]
    v = v.reshape(lead + (2, SSD_GROUPS, GROUP_HEADS))
    v = jnp.moveaxis(v, -3, -2).reshape(lead + (SSD_GROUPS, DT_SLOTS))
    v = jnp.tile(v, DT_COPIES)
    v = jnp.pad(v, [(0, 0)] * len(lead) + [(0, 0), (0, LANES - DT_COPIES * DT_SLOTS)])
    return v.reshape(lead + (DT_W,))


def _prep_w_in(w):
    head = 3 * ATTN_WIDTH + SSD_INNER + SSD_XBC
    dt_cols = _regroup_dt(w[:, head:head + 2 * SSD_HEADS])
    return jnp.concatenate([w[:, :head], dt_cols, w[:, head + 2 * SSD_HEADS:]], axis=1).astype(BF16)


def kernel(x, rel_table, norm1_g, w_in, q_norm_g, k_norm_g, ssd_conv_w, ssd_conv_b, ssd_dt_bias, ssd_a_log, ssd_d,
           ssd_norm_g, sc_conv_w, sc_conv_b, w_out, norm2_g, ffn_w_up, ffn_conv_w, ffn_conv_b, ffn_w_down):
    batch, s_len, d_model = x.shape
    assert d_model == D_MODEL and s_len == DILATED_PATTERNS[-1][0]
    assert s_len % OUT_TM == 0 and s_len % FFN_TM == 0 and s_len % IN_TM == 0
    depth = w_in.shape[0]
    x2 = x.reshape(batch * s_len, d_model)
    bias_wide, bias_narrow = _bias_tables(rel_table)
    row = lambda v: v.reshape(1, -1).astype(F32)
    for i in range(depth):
        qkv, z, xbc, dt, sc = _in_proj(x2, row(norm1_g[i]), _prep_w_in(w_in[i]), ssd_conv_w[i],
                                       row(ssd_conv_b[i]), s_len)
        gq = row(jnp.tile(q_norm_g[i], 2)) * (HEAD_DIM ** -0.5)
        gk = row(jnp.tile(k_norm_g[i], 2))
        a_out = _attention(qkv, gq, gk, bias_wide, bias_narrow, batch, s_len)
        s_out = _ssd(xbc, z, dt,
                     row(_regroup_dt(ssd_dt_bias[i].reshape(-1))), row(_regroup_dt(ssd_a_log[i].reshape(-1))),
                     row(jnp.repeat(ssd_d[i], SSD_HEADDIM)), row(ssd_norm_g[i]), batch, s_len)
        x2 = _out_proj(a_out, s_out, sc, x2, w_out[i].astype(BF16), sc_conv_w[i], row(sc_conv_b[i]), s_len)
        x2 = _ffn(x2, row(norm2_g[i]), ffn_w_up[i].astype(BF16), ffn_conv_w[i], row(ffn_conv_b[i]),
                  ffn_w_down[i].astype(BF16), s_len)
    return x2.reshape(batch, s_len, d_model)
```
